```python
import math
import jax
import jax.numpy as jnp
from jax import lax
import numpy as np

D_MODEL = 1024
BATCH = 16
SEQ = 2048
DEPTH = 2
DEC_BATCH = 128
DEC_SEQ = 1
PAST_LEN = 16384
PAGE_SIZE = 128

N_AB = (DEPTH + 1) // 2
N_C = DEPTH // 2

CONV_WIDTH = 4
GDN_HEADS = 4
GDN_DK = 128
GDN_DV = 128
GDN_CHUNK = 64
GDN_QK_W = GDN_HEADS * GDN_DK
GDN_QKV_W = 2 * GDN_QK_W + GDN_HEADS * GDN_DV
GDN_V_W = GDN_HEADS * GDN_DV
LRU_WIDTH = D_MODEL // 2
LRU_BLOCKS = 8
LRU_BW = LRU_WIDTH // LRU_BLOCKS
LRU_C = 8.0
AB_IN_W = GDN_QKV_W + GDN_V_W + 2 * GDN_HEADS + 2 * LRU_WIDTH
AB_MIX_W = GDN_V_W + LRU_WIDTH
SWA_HEADS = 16
SWA_KV_HEADS = 4
SWA_GROUP = SWA_HEADS // SWA_KV_HEADS
SWA_HEAD_DIM = 64
WINDOW = 128
SWA_KV_W = SWA_KV_HEADS * SWA_HEAD_DIM
SWA_OUT_W = SWA_HEADS * SWA_HEAD_DIM
SWA_QKV_W = SWA_OUT_W + 2 * SWA_KV_W
D_FF = -(-8 * D_MODEL // (3 * 256)) * 256
NORM_EPS = 1e-6

kernel_name = 'hybrid_gdn_rglru_swa_decoder_step'


def rmsnorm(x, w):
    xf = x.astype(jnp.float32)
    y = xf * lax.rsqrt(jnp.mean(xf * xf, axis=-1, keepdims=True) + NORM_EPS)
    return (y * w.astype(jnp.float32)).astype(x.dtype)


def l2norm(x):
    return x * lax.rsqrt(jnp.sum(x * x, axis=-1, keepdims=True) + NORM_EPS)


def causal_conv(x, buf, w, b=None):
    xp = jnp.concatenate([buf.astype(x.dtype), x], axis=1)
    y = lax.conv_general_dilated(xp, w[:, None, :].astype(x.dtype), window_strides=(1,), padding='VALID',
                                 dimension_numbers=('NWC', 'WIO', 'NWC'), feature_group_count=x.shape[-1])
    if b is not None:
        y = y + b.astype(x.dtype)
    return y, xp[:, xp.shape[1] - (CONV_WIDTH - 1):]


def gdn_chunked(q, k, v, beta, g, s0):
    bsz, t, h, dk = q.shape
    dv = v.shape[-1]
    c = GDN_CHUNK
    n = t // c

    def blk(a):
        return jnp.moveaxis(a.reshape((bsz, n, c, h) + a.shape[3:]), 3, 1)

    q, k, v, beta, g = blk(q), blk(k), blk(v), blk(beta), blk(g)
    gam = jnp.cumsum(g, axis=-1)
    causal = jnp.tril(jnp.ones((c, c), dtype=bool))
    strict = jnp.tril(jnp.ones((c, c), dtype=bool), -1)
    decay = jnp.exp(jnp.where(causal, gam[..., :, None] - gam[..., None, :], -jnp.inf))
    kk = jnp.einsum('bhnid,bhnjd->bhnij', k, k)
    a_mat = jnp.where(strict, beta[..., None] * kk * decay, 0.0) + jnp.eye(c, dtype=q.dtype)
    rhs = jnp.concatenate([beta[..., None] * v, (beta * jnp.exp(gam))[..., None] * k], axis=-1)
    sol = lax.linalg.triangular_solve(a_mat, rhs, left_side=True, lower=True, unit_diagonal=True)
    w_val, w_key = sol[..., :dv], sol[..., dv:]
    qk = jnp.einsum('bhnid,bhnjd->bhnij', q, k) * decay
    q_dec = q * jnp.exp(gam)[..., None]
    k_tail = k * jnp.exp(gam[..., -1:] - gam)[..., None]
    g_tot = jnp.exp(gam[..., -1])

    def step(s, xs):
        wv, wk, qkn, qd, kt, gt = xs
        u = wv - jnp.einsum('bhcd,bhde->bhce', wk, s)
        o = jnp.einsum('bhcd,bhde->bhce', qd, s) + jnp.einsum('bhcj,bhje->bhce', qkn, u)
        s = s * gt[..., None, None] + jnp.einsum('bhcd,bhce->bhde', kt, u)
        return s, o

    xs = tuple(jnp.moveaxis(a, 2, 0) for a in (w_val, w_key, qk, q_dec, k_tail, g_tot))
    s_fin, o = lax.scan(step, s0, xs)
    o = jnp.transpose(o, (1, 0, 3, 2, 4)).reshape(bsz, t, h, dv)
    return o, s_fin


def gdn_stepwise(q, k, v, beta, g, s0):
    def step(s, xs):
        qt, kt, vt, bt, gt = xs
        s = s * jnp.exp(gt)[..., None, None]
        pred = jnp.einsum('bhd,bhde->bhe', kt, s)
        s = s + jnp.einsum('bhd,bhe->bhde', kt, bt[..., None] * (vt - pred))
        return s, jnp.einsum('bhd,bhde->bhe', qt, s)

    xs = tuple(jnp.swapaxes(a, 0, 1) for a in (q, k, v, beta, g))
    s_fin, o = lax.scan(step, s0, xs)
    return jnp.swapaxes(o, 0, 1), s_fin


def rglru(x, h0, wa, ba, wx, bx, lam):
    bsz, t, w = x.shape
    xb = x.reshape(bsz, t, LRU_BLOCKS, LRU_BW)
    r = jax.nn.sigmoid(jnp.einsum('btni,nij->btnj', xb, wa).reshape(bsz, t, w) + ba)
    i = jax.nn.sigmoid(jnp.einsum('btni,nij->btnj', xb, wx).reshape(bsz, t, w) + bx)
    log_a = -LRU_C * r * jax.nn.softplus(-lam)
    a = jnp.exp(log_a)
    b = jnp.sqrt(-jnp.expm1(2.0 * log_a)) * (i * x)

    def comb(lhs, rhs):
        return lhs[0] * rhs[0], rhs[0] * lhs[1] + rhs[1]

    a_cum, hs = lax.associative_scan(comb, (a, b), axis=1)
    hs = hs + a_cum * h0[:, None, :]
    return hs, hs[:, -1]


def mixer_ab(hn, w_in, conv_gdn_w, a_log, dt_bias, gdn_norm_w, conv_lru_w, conv_lru_b, lru_wa, lru_ba,
             lru_wx, lru_bx, lru_lambda, w_out, s0, gdn_buf, h0, lru_buf, chunked):
    f32 = jnp.float32
    bsz, t, _ = hn.shape
    c1 = GDN_QKV_W
    c2 = c1 + GDN_V_W
    c3 = c2 + GDN_HEADS
    c4 = c3 + GDN_HEADS
    c5 = c4 + LRU_WIDTH
    proj = hn @ w_in
    qkv, z, b_in, a_in, gate_in, xr_in = jnp.split(proj, [c1, c2, c3, c4, c5], axis=-1)
    qkv, gdn_buf_new = causal_conv(qkv, gdn_buf, conv_gdn_w)
    qkv = jax.nn.silu(qkv).astype(f32)
    q = l2norm(qkv[..., :GDN_QK_W].reshape(bsz, t, GDN_HEADS, GDN_DK)) * GDN_DK ** -0.5
    k = l2norm(qkv[..., GDN_QK_W:2 * GDN_QK_W].reshape(bsz, t, GDN_HEADS, GDN_DK))
    v = qkv[..., 2 * GDN_QK_W:].reshape(bsz, t, GDN_HEADS, GDN_DV)
    beta = jax.nn.sigmoid(b_in.astype(f32))
    g = -jnp.exp(a_log.astype(f32)) * jax.nn.softplus(a_in.astype(f32) + dt_bias.astype(f32))
    core = gdn_chunked if chunked else gdn_stepwise
    o, s_new = core(q, k, v, beta, g, s0.astype(f32))
    o = o * lax.rsqrt(jnp.mean(o * o, axis=-1, keepdims=True) + NORM_EPS) * gdn_norm_w.astype(f32)
    o = o * jax.nn.silu(z.astype(f32).reshape(bsz, t, GDN_HEADS, GDN_DV))
    o = o.reshape(bsz, t, GDN_V_W).astype(hn.dtype)
    xr, lru_buf_new = causal_conv(xr_in, lru_buf, conv_lru_w, conv_lru_b)
    hs, h_new = rglru(xr.astype(f32), h0.astype(f32), lru_wa.astype(f32), lru_ba.astype(f32),
                      lru_wx.astype(f32), lru_bx.astype(f32), lru_lambda.astype(f32))
    y_lru = jax.nn.gelu(gate_in) * hs.astype(hn.dtype)
    out = jnp.concatenate([o, y_lru], axis=-1) @ w_out
    return out, s_new.astype(hn.dtype), gdn_buf_new, h_new.astype(hn.dtype), lru_buf_new


def alibi_slopes():
    return jnp.exp2(-8.0 * jnp.arange(1, SWA_HEADS + 1, dtype=jnp.float32) / SWA_HEADS)


def window_attend(qg, kk, vv, rel, keep, sinks, slopes):
    f32 = jnp.float32
    s = jnp.einsum('bqhgd,bkhd->bhgqk', qg, kk).astype(f32) * SWA_HEAD_DIM ** -0.5
    s = s - slopes.reshape(SWA_KV_HEADS, SWA_GROUP, 1, 1) * rel.astype(f32)
    s = jnp.where(keep, s, -jnp.inf)
    sink = sinks.astype(f32).reshape(SWA_KV_HEADS, SWA_GROUP, 1, 1)
    m = jnp.maximum(jnp.max(s, axis=-1, keepdims=True), sink)
    p = jnp.exp(s - m)
    p = p / (jnp.sum(p, axis=-1, keepdims=True) + jnp.exp(sink - m))
    return jnp.einsum('bhgqk,bkhd->bqhgd', p.astype(vv.dtype), vv)


def mixer_c(hn, w_qkv, w_out, sinks, k_buf, v_buf):
    bsz, t, _ = hn.shape
    qkv = hn @ w_qkv
    q = qkv[..., :SWA_OUT_W].reshape(bsz, t, SWA_KV_HEADS, SWA_GROUP, SWA_HEAD_DIM)
    k = qkv[..., SWA_OUT_W:SWA_OUT_W + SWA_KV_W].reshape(bsz, t, SWA_KV_HEADS, SWA_HEAD_DIM)
    v = qkv[..., SWA_OUT_W + SWA_KV_W:].reshape(bsz, t, SWA_KV_HEADS, SWA_HEAD_DIM)
    slopes = alibi_slopes()
    if k_buf is None:
        nb = t // WINDOW
        qb = jnp.moveaxis(q.reshape(bsz, nb, WINDOW, SWA_KV_HEADS, SWA_GROUP, SWA_HEAD_DIM), 1, 0)

        def band(a):
            ab = jnp.moveaxis(a.reshape(bsz, nb, WINDOW, SWA_KV_HEADS, SWA_HEAD_DIM), 1, 0)
            prev = jnp.concatenate([jnp.zeros_like(ab[:1]), ab[:-1]], axis=0)
            return jnp.concatenate([prev, ab], axis=2)

        kb, vb = band(k), band(v)
        kj = jnp.arange(2 * WINDOW)[None, :]
        rel = jnp.arange(WINDOW)[:, None] + WINDOW - kj
        in_band = (rel >= 0) & (rel <= WINDOW)

        def one_block(args):
            qn, kn, vn, n = args
            keep = in_band & (n * WINDOW + kj - WINDOW >= 0)
            return window_attend(qn, kn, vn, rel, keep, sinks, slopes)

        o = lax.map(one_block, (qb, kb, vb, jnp.arange(nb)))
        o = jnp.moveaxis(o, 0, 1).reshape(bsz, t, SWA_OUT_W)
        k_new, v_new = k[:, t - WINDOW:], v[:, t - WINDOW:]
    else:
        lb = k_buf.shape[1]
        kk = jnp.concatenate([k_buf.astype(k.dtype), k], axis=1)
        vv = jnp.concatenate([v_buf.astype(v.dtype), v], axis=1)
        rel = (lb + jnp.arange(t))[:, None] - jnp.arange(lb + t)[None, :]
        keep = (rel >= 0) & (rel <= WINDOW)
        o = window_attend(q, kk, vv, rel, keep, sinks, slopes).reshape(bsz, t, SWA_OUT_W)
        k_new, v_new = kk[:, t:], vv[:, t:]
    return o @ w_out, k_new, v_new


def swiglu(hn, w_gate_up, w_down):
    gu = hn @ w_gate_up
    return (jax.nn.silu(gu[..., :D_FF]) * gu[..., D_FF:]) @ w_down


def run_trunk(x, gdn_s, gdn_cb, lru_h, lru_cb, swa_k, swa_v, prm, is_prompt):
    n_s, n_gcb, n_h, n_lcb, n_k, n_v = [], [], [], [], [], []
    ia = 0
    ic = 0
    for layer in range(DEPTH):
        hn = rmsnorm(x, prm['norm_mix'][layer])
        if layer % 2 == 0:
            out, s_new, gcb, h_new, lcb = mixer_ab(
                hn, prm['w_in_ab'][ia], prm['conv_gdn_w'][ia], prm['gdn_a_log'][ia], prm['gdn_dt_bias'][ia],
                prm['gdn_norm_w'][ia], prm['conv_lru_w'][ia], prm['conv_lru_b'][ia], prm['lru_wa'][ia],
                prm['lru_ba'][ia], prm['lru_wx'][ia], prm['lru_bx'][ia], prm['lru_lambda'][ia], prm['w_out_ab'][ia],
                gdn_s[ia], gdn_cb[ia], lru_h[ia], lru_cb[ia], is_prompt)
            n_s.append(s_new)
            n_gcb.append(gcb)
            n_h.append(h_new)
            n_lcb.append(lcb)
            ia += 1
        else:
            kb = None if is_prompt else swa_k[ic]
            vb = None if is_prompt else swa_v[ic]
            out, k_new, v_new = mixer_c(hn, prm['w_qkv_c'][ic], prm['w_out_c'][ic], prm['sinks_c'][ic], kb, vb)
            n_k.append(k_new)
            n_v.append(v_new)
            ic += 1
        x = x + out
        x = x + swiglu(rmsnorm(x, prm['norm_ffn'][layer]), prm['w_gate_up'][layer], prm['w_down'][layer])
    y = rmsnorm(x, prm['norm_final'])
    return (y, jnp.stack(n_s), jnp.stack(n_gcb), jnp.stack(n_h), jnp.stack(n_lcb),
            jnp.stack(n_k), jnp.stack(n_v))


def setup_inputs(seed: int = 0) -> dict:
    key = jax.random.key(seed)
    sub = jax.random.split(key, 32)
    idx = list(range(32))
    f32 = jnp.float32

    def nk():
        return sub[idx.pop()]

    def nrm(shape, scale):
        return jax.random.normal(nk(), shape, f32) * scale

    def gain(shape):
        return 1.0 + nrm(shape, 0.02)

    lb = min(WINDOW, PAST_LEN)
    dt = jnp.exp(jax.random.uniform(nk(), (N_AB, GDN_HEADS), f32, math.log(1e-3), math.log(1e-1)))
    a_pow = jax.random.uniform(nk(), (N_AB, LRU_WIDTH), f32, 0.9, 0.999)
    a_base = a_pow ** (1.0 / LRU_C)
    return {
        'x_prompt': nrm((BATCH, SEQ, D_MODEL), 1.0),
        'x_sample': nrm((DEC_BATCH, DEC_SEQ, D_MODEL), 1.0),
        'state_gdn': nrm((N_AB, DEC_BATCH, GDN_HEADS, GDN_DK, GDN_DV), 0.3),
        'state_gdn_conv': nrm((N_AB, DEC_BATCH, CONV_WIDTH - 1, GDN_QKV_W), 1.0),
        'state_lru': nrm((N_AB, DEC_BATCH, LRU_WIDTH), 0.5),
        'state_lru_conv': nrm((N_AB, DEC_BATCH, CONV_WIDTH - 1, LRU_WIDTH), 1.0),
        'cache_swa_k': nrm((N_C, DEC_BATCH, lb, SWA_KV_HEADS, SWA_HEAD_DIM), 1.0),
        'cache_swa_v': nrm((N_C, DEC_BATCH, lb, SWA_KV_HEADS, SWA_HEAD_DIM), 1.0),
        'norm_mix': gain((DEPTH, D_MODEL)),
        'norm_ffn': gain((DEPTH, D_MODEL)),
        'norm_final': gain((D_MODEL,)),
        'w_in_ab': nrm((N_AB, D_MODEL, AB_IN_W), D_MODEL ** -0.5),
        'conv_gdn_w': nrm((N_AB, CONV_WIDTH, GDN_QKV_W), CONV_WIDTH ** -0.5),
        'gdn_a_log': jnp.log(jax.random.uniform(nk(), (N_AB, GDN_HEADS), f32, 1.0, 16.0)),
        'gdn_dt_bias': dt + jnp.log(-jnp.expm1(-dt)),
        'gdn_norm_w': gain((N_AB, GDN_DV)),
        'conv_lru_w': nrm((N_AB, CONV_WIDTH, LRU_WIDTH), CONV_WIDTH ** -0.5),
        'conv_lru_b': nrm((N_AB, LRU_WIDTH), 0.02),
        'lru_wa': nrm((N_AB, LRU_BLOCKS, LRU_BW, LRU_BW), LRU_BW ** -0.5),
        'lru_ba': nrm((N_AB, LRU_WIDTH), 0.02),
        'lru_wx': nrm((N_AB, LRU_BLOCKS, LRU_BW, LRU_BW), LRU_BW ** -0.5),
        'lru_bx': nrm((N_AB, LRU_WIDTH), 0.02),
        'lru_lambda': jnp.log(a_base) - jnp.log1p(-a_base),
        'w_out_ab': nrm((N_AB, AB_MIX_W, D_MODEL), AB_MIX_W ** -0.5),
        'w_qkv_c': nrm((N_C, D_MODEL, SWA_QKV_W), D_MODEL ** -0.5),
        'w_out_c': nrm((N_C, SWA_OUT_W, D_MODEL), SWA_OUT_W ** -0.5),
        'sinks_c': nrm((N_C, SWA_HEADS), 1.0),
        'w_gate_up': nrm((DEPTH, D_MODEL, 2 * D_FF), D_MODEL ** -0.5),
        'w_down': nrm((DEPTH, D_FF, D_MODEL), D_FF ** -0.5),
    }


def reference(x_prompt, x_sample, state_gdn, state_gdn_conv, state_lru, state_lru_conv, cache_swa_k, cache_swa_v,
              norm_mix, norm_ffn, norm_final, w_in_ab, conv_gdn_w, gdn_a_log, gdn_dt_bias, gdn_norm_w,
              conv_lru_w, conv_lru_b, lru_wa, lru_ba, lru_wx, lru_bx, lru_lambda, w_out_ab, w_qkv_c, w_out_c,
              sinks_c, w_gate_up, w_down):
    prm = dict(norm_mix=norm_mix, norm_ffn=norm_ffn, norm_final=norm_final, w_in_ab=w_in_ab,
               conv_gdn_w=conv_gdn_w, gdn_a_log=gdn_a_log, gdn_dt_bias=gdn_dt_bias, gdn_norm_w=gdn_norm_w,
               conv_lru_w=conv_lru_w, conv_lru_b=conv_lru_b, lru_wa=lru_wa, lru_ba=lru_ba, lru_wx=lru_wx,
               lru_bx=lru_bx, lru_lambda=lru_lambda, w_out_ab=w_out_ab, w_qkv_c=w_qkv_c, w_out_c=w_out_c,
               sinks_c=sinks_c, w_gate_up=w_gate_up, w_down=w_down)
    bp = x_prompt.shape[0]
    dt = x_prompt.dtype
    z_s = jnp.zeros((N_AB, bp, GDN_HEADS, GDN_DK, GDN_DV), dt)
    z_gcb = jnp.zeros((N_AB, bp, CONV_WIDTH - 1, GDN_QKV_W), dt)
    z_h = jnp.zeros((N_AB, bp, LRU_WIDTH), dt)
    z_lcb = jnp.zeros((N_AB, bp, CONV_WIDTH - 1, LRU_WIDTH), dt)
    y_prompt, p_gdn, p_gdn_conv, p_lru, p_lru_conv, p_swa_k, p_swa_v = run_trunk(
        x_prompt, z_s, z_gcb, z_h, z_lcb, None, None, prm, True)
    y_sample, s_gdn, s_gdn_conv, s_lru, s_lru_conv, s_swa_k, s_swa_v = run_trunk(
        x_sample, state_gdn, state_gdn_conv, state_lru, state_lru_conv, cache_swa_k, cache_swa_v, prm, False)
    return (y_prompt, y_sample, p_gdn, p_gdn_conv, p_lru, p_lru_conv, p_swa_k, p_swa_v,
            s_gdn, s_gdn_conv, s_lru, s_lru_conv, s_swa_k, s_swa_v)
```

```python
import functools
import math

import jax
import jax.numpy as jnp
from jax import lax
from jax.experimental import pallas as pl
from jax.experimental.pallas import tpu as pltpu

F32 = jnp.float32
BF16 = jnp.bfloat16

NORM_EPS = 1e-6
CONV_WIDTH = 4
CONV_TAIL = 8
GDN_HEADS = 4
GDN_DK = 128
GDN_DV = 128
GDN_CHUNK = 64
GDN_QK_W = GDN_HEADS * GDN_DK
GDN_V_W = GDN_HEADS * GDN_DV
GDN_QKV_W = 2 * GDN_QK_W + GDN_V_W
LRU_BLOCKS = 8
LRU_C = 8.0
SWA_HEADS = 16
SWA_KV_HEADS = 4
SWA_GROUP = SWA_HEADS // SWA_KV_HEADS
SWA_HEAD_DIM = 64
WINDOW = 128
SWA_KV_W = SWA_KV_HEADS * SWA_HEAD_DIM
SWA_OUT_W = SWA_HEADS * SWA_HEAD_DIM
LANES = 128
VMEM_LIMIT_BYTES = 56 * 1024 * 1024


def _params(*semantics):
    return pltpu.CompilerParams(dimension_semantics=semantics, vmem_limit_bytes=VMEM_LIMIT_BYTES)


def _sigmoid(x):
    return 1.0 / (1.0 + jnp.exp(-x))


def _silu(x):
    return x * _sigmoid(x)


def _softplus(x):
    return jnp.maximum(x, 0.0) + jnp.log1p(jnp.exp(-jnp.abs(x)))


def _gelu_tanh(x):
    return 0.5 * x * (1.0 + jnp.tanh(math.sqrt(2.0 / math.pi) * (x + 0.044715 * (x * x * x))))


def _rms(x, w):
    return x * lax.rsqrt(jnp.mean(x * x, axis=-1, keepdims=True) + NORM_EPS) * w


def _dot(a, b):
    return jnp.dot(a.astype(BF16), b.astype(BF16), preferred_element_type=F32)


def _dot_nt(a, b):
    return lax.dot_general(a.astype(BF16), b.astype(BF16), (((1,), (1,)), ((), ())), preferred_element_type=F32)


def _dot_tn(a, b):
    return lax.dot_general(a.astype(BF16), b.astype(BF16), (((0,), (0,)), ((), ())), preferred_element_type=F32)


def _norm_matmul_body(x_ref, nw_ref, w_ref, *o_refs):
    hb = _rms(x_ref[...], nw_ref[...]).astype(BF16)
    off = 0
    for o_ref in o_refs:
        n = o_ref.shape[-1]
        o_ref[...] = jnp.dot(hb, w_ref[:, off:off + n], preferred_element_type=F32)
        off += n


def _norm_matmul(x, nw, w, splits, tm):
    m, d = x.shape
    n = w.shape[1]
    assert sum(splits) == n and m % tm == 0
    return pl.pallas_call(
        _norm_matmul_body,
        grid=(m // tm,),
        in_specs=[pl.BlockSpec((tm, d), lambda i: (i, 0)),
                  pl.BlockSpec((1, d), lambda i: (0, 0)),
                  pl.BlockSpec((d, n), lambda i: (0, 0))],
        out_specs=[pl.BlockSpec((tm, s), lambda i: (i, 0)) for s in splits],
        out_shape=[jax.ShapeDtypeStruct((m, s), F32) for s in splits],
        compiler_params=_params("parallel"),
    )(x, nw, w)


def _proj_residual_body(n_in, res_ref, *refs):
    a_refs, w_refs, o_ref = refs[:n_in], refs[n_in:2 * n_in], refs[2 * n_in]
    acc = res_ref[...]
    for a_ref, w_ref in zip(a_refs, w_refs):
        acc = acc + jnp.dot(a_ref[...].astype(BF16), w_ref[...], preferred_element_type=F32)
    o_ref[...] = acc


def _proj_residual(res, acts, ws, tm):
    m, d = res.shape
    assert m % tm == 0
    return pl.pallas_call(
        functools.partial(_proj_residual_body, len(acts)),
        grid=(m // tm,),
        in_specs=([pl.BlockSpec((tm, d), lambda i: (i, 0))]
                  + [pl.BlockSpec((tm, a.shape[1]), lambda i: (i, 0)) for a in acts]
                  + [pl.BlockSpec(w.shape, lambda i: (0, 0)) for w in ws]),
        out_specs=pl.BlockSpec((tm, d), lambda i: (i, 0)),
        out_shape=jax.ShapeDtypeStruct((m, d), F32),
        compiler_params=_params("parallel"),
    )(res, *acts, *ws)


def _ffn_body(final_norm, x_ref, nw_ref, wg_ref, wu_ref, wd_ref, fnw_ref, o_ref, h_scr, acc_scr):
    j = pl.program_id(1)

    @pl.when(j == 0)
    def _():
        h_scr[...] = _rms(x_ref[...], nw_ref[...]).astype(BF16)
        acc_scr[...] = jnp.zeros_like(acc_scr)

    hb = h_scr[...]
    g = jnp.dot(hb, wg_ref[...], preferred_element_type=F32)
    u = jnp.dot(hb, wu_ref[...], preferred_element_type=F32)
    acc_scr[...] += jnp.dot((_silu(g) * u).astype(BF16), wd_ref[...], preferred_element_type=F32)

    @pl.when(j == pl.num_programs(1) - 1)
    def _():
        y = x_ref[...] + acc_scr[...]
        if final_norm:
            y = _rms(y, fnw_ref[...])
        o_ref[...] = y


def _ffn(x, nw, w_gate_up, w_down, fnw, final_norm, tm, tf):
    m, d = x.shape
    d_ff = w_down.shape[0]
    assert m % tm == 0 and d_ff % tf == 0
    nf = d_ff // tf
    return pl.pallas_call(
        functools.partial(_ffn_body, final_norm),
        grid=(m // tm, nf),
        in_specs=[pl.BlockSpec((tm, d), lambda i, j: (i, 0)),
                  pl.BlockSpec((1, d), lambda i, j: (0, 0)),
                  pl.BlockSpec((d, tf), lambda i, j: (0, j)),
                  pl.BlockSpec((d, tf), lambda i, j: (0, j + nf)),
                  pl.BlockSpec((tf, d), lambda i, j: (j, 0)),
                  pl.BlockSpec((1, d), lambda i, j: (0, 0))],
        out_specs=pl.BlockSpec((tm, d), lambda i, j: (i, 0)),
        out_shape=jax.ShapeDtypeStruct((m, d), F32),
        scratch_shapes=[pltpu.VMEM((tm, d), BF16), pltpu.VMEM((tm, d), F32)],
        compiler_params=_params("parallel", "arbitrary"),
    )(x, nw, w_gate_up, w_gate_up, w_down, fnw)


def _conv_tile(x_ref_rows, w_ref, xp_scr, first_tile):
    tt = x_ref_rows.shape[0]

    @pl.when(first_tile)
    def _():
        xp_scr[0:CONV_TAIL, :] = jnp.zeros((CONV_TAIL, xp_scr.shape[1]), F32)

    xp_scr[CONV_TAIL:CONV_TAIL + tt, :] = x_ref_rows
    y = w_ref[CONV_WIDTH - 1:CONV_WIDTH, :] * xp_scr[CONV_TAIL:CONV_TAIL + tt, :]
    for j in range(CONV_WIDTH - 1):
        off = CONV_TAIL - (CONV_WIDTH - 1) + j
        y = y + w_ref[j:j + 1, :] * xp_scr[off:off + tt, :]
    xp_scr[0:CONV_TAIL, :] = xp_scr[tt:tt + CONV_TAIL, :]
    return y


def _gdn_prompt_body(qkv_ref, z_ref, ba_ref, cw_ref, alog_ref, dtb_ref, gnw_ref, o_ref, s_ref,
                     xp_scr, q_scr, k_scr, v_scr, bg_scr, gam_scr, gamt_scr):
    t = pl.program_id(1)
    tt = qkv_ref.shape[1]
    c = GDN_CHUNK
    nc = tt // c
    h_n = GDN_HEADS

    @pl.when(t == 0)
    def _():
        s_ref[...] = jnp.zeros_like(s_ref)

    act = _silu(_conv_tile(qkv_ref[0], cw_ref, xp_scr, t == 0))
    for h in range(h_n):
        lo, hi = h * GDN_DK, (h + 1) * GDN_DK
        qh = act[:, lo:hi]
        q_scr[:, lo:hi] = qh * lax.rsqrt(jnp.sum(qh * qh, axis=-1, keepdims=True) + NORM_EPS) * GDN_DK ** -0.5
        kh = act[:, GDN_QK_W + lo:GDN_QK_W + hi]
        k_scr[:, lo:hi] = kh * lax.rsqrt(jnp.sum(kh * kh, axis=-1, keepdims=True) + NORM_EPS)
    v_scr[...] = act[:, 2 * GDN_QK_W:]

    ba = ba_ref[0]
    lane = lax.broadcasted_iota(jnp.int32, ba.shape, 1)
    g = -jnp.exp(alog_ref[...]) * _softplus(ba + dtb_ref[...])
    bg_scr[...] = jnp.where(lane < h_n, _sigmoid(ba), g)
    row = lax.broadcasted_iota(jnp.int32, ba.shape, 0) % c
    gam = g
    shift = 1
    while shift < c:
        gam = gam + jnp.where(row >= shift, pltpu.roll(gam, shift, axis=0), 0.0)
        shift *= 2
    gam_scr[...] = gam
    for ci in range(nc):
        gamt_scr[ci] = gam[ci * c:(ci + 1) * c, :].T

    ri = lax.broadcasted_iota(jnp.int32, (c, c), 0)
    cj = lax.broadcasted_iota(jnp.int32, (c, c), 1)
    causal = ri >= cj
    strict = ri > cj

    def chunk(ci, carry):
        r0 = pl.multiple_of(ci * c, c)
        bgc = bg_scr[pl.ds(r0, c), :]
        gamc = gam_scr[pl.ds(r0, c), :]
        gamt = gamt_scr[ci]
        for h in range(h_n):
            lo, hi = h * GDN_DK, (h + 1) * GDN_DK
            beta_c = bgc[:, h:h + 1]
            gam_c = gamc[:, h_n + h:h_n + h + 1]
            gam_r = gamt[h_n + h:h_n + h + 1, :]
            gam_last = gamc[c - 1:c, h_n + h:h_n + h + 1]
            q = q_scr[pl.ds(r0, c), lo:hi]
            k = k_scr[pl.ds(r0, c), lo:hi]
            v = v_scr[pl.ds(r0, c), lo:hi]
            decay = jnp.exp(jnp.where(causal, gam_c - gam_r, -jnp.inf))
            kk = _dot_nt(k, k)
            qk = _dot_nt(q, k) * decay
            pw = jnp.where(strict, -(beta_c * kk * decay), 0.0)
            tq = pw
            for _ in range(int(math.log2(c)) - 1):
                pw = _dot(pw, pw)
                tq = tq + pw + _dot(tq, pw)
            e_gam = jnp.exp(gam_c)
            rhs = jnp.concatenate([beta_c * v, (beta_c * e_gam) * k], axis=1)
            sol = rhs + _dot(tq, rhs)
            w_val, w_key = sol[:, :GDN_DV], sol[:, GDN_DV:]
            q_dec = q * e_gam
            k_tail = k * jnp.exp(gam_last - gam_c)
            s_old = s_ref[0, h]
            u = w_val - _dot(w_key, s_old)
            o = _dot(q_dec, s_old) + _dot(qk, u)
            s_ref[0, h] = s_old * jnp.exp(gam_last) + _dot_tn(k_tail, u)
            o = o * lax.rsqrt(jnp.mean(o * o, axis=-1, keepdims=True) + NORM_EPS) * gnw_ref[...]
            o_ref[0, pl.ds(r0, c), lo:hi] = o * _silu(z_ref[0, pl.ds(r0, c), lo:hi])
        return carry

    lax.fori_loop(0, nc, chunk, 0)


def _gdn_prompt(qkv, z, ba, conv_w, alog_row, dtb_row, gnorm_w, tt):
    b, t, _ = qkv.shape
    assert t % tt == 0 and tt % GDN_CHUNK == 0
    nc = tt // GDN_CHUNK
    row_spec = lambda w: pl.BlockSpec((1, tt, w), lambda i, j: (i, j, 0))
    full = lambda a: pl.BlockSpec(a.shape, lambda i, j: (0,) * a.ndim)
    return pl.pallas_call(
        _gdn_prompt_body,
        grid=(b, t // tt),
        in_specs=[row_spec(GDN_QKV_W), row_spec(GDN_V_W), row_spec(LANES),
                  full(conv_w), full(alog_row), full(dtb_row), full(gnorm_w)],
        out_specs=[row_spec(GDN_V_W),
                   pl.BlockSpec((1, GDN_HEADS, GDN_DK, GDN_DV), lambda i, j: (i, 0, 0, 0))],
        out_shape=[jax.ShapeDtypeStruct((b, t, GDN_V_W), F32),
                   jax.ShapeDtypeStruct((b, GDN_HEADS, GDN_DK, GDN_DV), F32)],
        scratch_shapes=[pltpu.VMEM((CONV_TAIL + tt, GDN_QKV_W), F32),
                        pltpu.VMEM((tt, GDN_QK_W), F32), pltpu.VMEM((tt, GDN_QK_W), F32),
                        pltpu.VMEM((tt, GDN_V_W), F32),
                        pltpu.VMEM((tt, LANES), F32), pltpu.VMEM((tt, LANES), F32),
                        pltpu.VMEM((nc, LANES, GDN_CHUNK), F32)],
        compiler_params=_params("parallel", "arbitrary"),
    )(qkv, z, ba, conv_w, alog_row, dtb_row, gnorm_w)


def _lru_gates(x, wg_ref, bg_ref, lam_ref):
    w = x.shape[1]
    gates = jnp.dot(x.astype(BF16), wg_ref[...], preferred_element_type=F32) + bg_ref[...]
    r = _sigmoid(gates[:, :w])
    i = _sigmoid(gates[:, w:])
    log_a = -LRU_C * r * _softplus(-lam_ref[...])
    a = jnp.exp(log_a)
    b = jnp.sqrt(jnp.tanh(-log_a) * (1.0 + a * a)) * (i * x)
    return a, b


def _lru_prompt_body(xr_ref, gate_ref, cw_ref, cb_ref, wg_ref, bg_ref, lam_ref, y_ref, h_ref, xp_scr):
    t = pl.program_id(1)
    tt = xr_ref.shape[1]

    @pl.when(t == 0)
    def _():
        h_ref[...] = jnp.zeros_like(h_ref)

    x = _conv_tile(xr_ref[0], cw_ref, xp_scr, t == 0) + cb_ref[...]
    a, b = _lru_gates(x, wg_ref, bg_ref, lam_ref)
    row = lax.broadcasted_iota(jnp.int32, a.shape, 0)
    shift = 1
    while shift < tt:
        keep = row >= shift
        a_prev = jnp.where(keep, pltpu.roll(a, shift, axis=0), 1.0)
        b_prev = jnp.where(keep, pltpu.roll(b, shift, axis=0), 0.0)
        b = a * b_prev + b
        a = a * a_prev
        shift *= 2
    hs = b + a * h_ref[0]
    h_ref[0] = hs[tt - 1:tt, :]
    y_ref[0] = _gelu_tanh(gate_ref[0]) * hs


def _lru_prompt(xr, gate, conv_w, conv_b, w_gates, b_gates, lam, tt):
    b, t, w = xr.shape
    assert t % tt == 0
    row_spec = pl.BlockSpec((1, tt, w), lambda i, j: (i, j, 0))
    full = lambda a: pl.BlockSpec(a.shape, lambda i, j: (0,) * a.ndim)
    return pl.pallas_call(
        _lru_prompt_body,
        grid=(b, t // tt),
        in_specs=[row_spec, row_spec, full(conv_w), full(conv_b), full(w_gates), full(b_gates), full(lam)],
        out_specs=[row_spec, pl.BlockSpec((1, 1, w), lambda i, j: (i, 0, 0))],
        out_shape=[jax.ShapeDtypeStruct((b, t, w), F32), jax.ShapeDtypeStruct((b, 1, w), F32)],
        scratch_shapes=[pltpu.VMEM((CONV_TAIL + tt, w), F32)],
        compiler_params=_params("parallel", "arbitrary"),
    )(xr, gate, conv_w, conv_b, w_gates, b_gates, lam)


def _swa_prompt_body(slope_ref, sink_ref, q_ref, kp_ref, kc_ref, vp_ref, vc_ref, o_ref):
    n = pl.program_id(1)
    w = WINDOW
    hd = SWA_HEAD_DIM
    qi = lax.broadcasted_iota(jnp.int32, (w, 2 * w), 0)
    kj = lax.broadcasted_iota(jnp.int32, (w, 2 * w), 1)
    rel = qi + w - kj
    keep = (rel >= 0) & (rel <= w) & (n * w + kj - w >= 0)
    relf = rel.astype(F32)
    for kh in range(SWA_KV_HEADS):
        kcat = jnp.concatenate([kp_ref[0, :, kh * hd:(kh + 1) * hd], kc_ref[0, :, kh * hd:(kh + 1) * hd]], axis=0)
        vcat = jnp.concatenate([vp_ref[0, :, kh * hd:(kh + 1) * hd], vc_ref[0, :, kh * hd:(kh + 1) * hd]], axis=0)
        for gi in range(SWA_GROUP):
            hh = kh * SWA_GROUP + gi
            s = _dot_nt(q_ref[0, :, hh * hd:(hh + 1) * hd], kcat) * hd ** -0.5
            s = jnp.where(keep, s - slope_ref[hh] * relf, -jnp.inf)
            sink = sink_ref[hh]
            m = jnp.maximum(jnp.max(s, axis=-1, keepdims=True), sink)
            p = jnp.exp(s - m)
            denom = jnp.sum(p, axis=-1, keepdims=True) + jnp.exp(sink - m)
            o_ref[0, :, hh * hd:(hh + 1) * hd] = _dot(p, vcat) / denom


def _swa_prompt(q, k, v, slopes, sinks):
    b, t, _ = q.shape
    nb = t // WINDOW
    smem = pl.BlockSpec(memory_space=pltpu.SMEM)
    prev = pl.BlockSpec((1, WINDOW, SWA_KV_W), lambda i, j: (i, jnp.maximum(j - 1, 0), 0))
    cur = pl.BlockSpec((1, WINDOW, SWA_KV_W), lambda i, j: (i, j, 0))
    return pl.pallas_call(
        _swa_prompt_body,
        grid=(b, nb),
        in_specs=[smem, smem, pl.BlockSpec((1, WINDOW, SWA_OUT_W), lambda i, j: (i, j, 0)), prev, cur, prev, cur],
        out_specs=pl.BlockSpec((1, WINDOW, SWA_OUT_W), lambda i, j: (i, j, 0)),
        out_shape=jax.ShapeDtypeStruct((b, t, SWA_OUT_W), F32),
        compiler_params=_params("parallel", "arbitrary"),
    )(slopes, sinks, q, k, k, v, v)


def _gdn_sample_body(qkv_ref, buf_ref, z_ref, ba_ref, s_in_ref, cw_ref, alog_ref, dtb_ref, gnw_ref,
                     o_ref, s_out_ref, o_scr):
    bt = qkv_ref.shape[0]
    wq = GDN_QKV_W
    y = cw_ref[CONV_WIDTH - 1:CONV_WIDTH, :] * qkv_ref[...]
    for j in range(CONV_WIDTH - 1):
        y = y + cw_ref[j:j + 1, :] * buf_ref[:, j * wq:(j + 1) * wq]
    act = _silu(y)
    ba = ba_ref[...]
    beta = _sigmoid(ba)
    eg = jnp.exp(-jnp.exp(alog_ref[...]) * _softplus(ba + dtb_ref[...]))
    pad = jnp.zeros((LANES - bt, GDN_DK), F32)
    for h in range(GDN_HEADS):
        lo, hi = h * GDN_DK, (h + 1) * GDN_DK
        qh = act[:, lo:hi]
        qh = qh * lax.rsqrt(jnp.sum(qh * qh, axis=-1, keepdims=True) + NORM_EPS) * GDN_DK ** -0.5
        kh = act[:, GDN_QK_W + lo:GDN_QK_W + hi]
        kh = kh * lax.rsqrt(jnp.sum(kh * kh, axis=-1, keepdims=True) + NORM_EPS)
        vh = act[:, 2 * GDN_QK_W + lo:2 * GDN_QK_W + hi]
        qt = jnp.concatenate([qh, pad], axis=0).T
        kt = jnp.concatenate([kh, pad], axis=0).T
        for i in range(bt):
            kcol = kt[:, i:i + 1]
            qcol = qt[:, i:i + 1]
            s_dec = s_in_ref[i, h] * eg[i:i + 1, GDN_HEADS + h:GDN_HEADS + h + 1]
            pred = jnp.sum(kcol * s_dec, axis=0, keepdims=True)
            upd = beta[i:i + 1, h:h + 1] * (vh[i:i + 1, :] - pred)
            s_new = s_dec + kcol * upd
            s_out_ref[i, h] = s_new
            o_scr[i:i + 1, lo:hi] = jnp.sum(qcol * s_new, axis=0, keepdims=True)
    for h in range(GDN_HEADS):
        lo, hi = h * GDN_DV, (h + 1) * GDN_DV
        o = o_scr[:, lo:hi]
        o = o * lax.rsqrt(jnp.mean(o * o, axis=-1, keepdims=True) + NORM_EPS) * gnw_ref[...]
        o_ref[:, lo:hi] = o * _silu(z_ref[:, lo:hi])


def _gdn_sample(qkv, buf, z, ba, state, conv_w, alog_row, dtb_row, gnorm_w, bt):
    b = qkv.shape[0]
    assert b % bt == 0
    row = lambda a: pl.BlockSpec((bt, a.shape[1]), lambda i: (i, 0))
    full = lambda a: pl.BlockSpec(a.shape, lambda i: (0,) * a.ndim)
    st = pl.BlockSpec((bt, GDN_HEADS, GDN_DK, GDN_DV), lambda i: (i, 0, 0, 0))
    return pl.pallas_call(
        _gdn_sample_body,
        grid=(b // bt,),
        in_specs=[row(qkv), row(buf), row(z), row(ba), st, full(conv_w), full(alog_row), full(dtb_row), full(gnorm_w)],
        out_specs=[pl.BlockSpec((bt, GDN_V_W), lambda i: (i, 0)), st],
        out_shape=[jax.ShapeDtypeStruct((b, GDN_V_W), F32), jax.ShapeDtypeStruct(state.shape, F32)],
        scratch_shapes=[pltpu.VMEM((bt, GDN_V_W), F32)],
        compiler_params=_params("parallel"),
    )(qkv, buf, z, ba, state, conv_w, alog_row, dtb_row, gnorm_w)


def _lru_sample_body(xr_ref, buf_ref, gate_ref, h0_ref, cw_ref, cb_ref, wg_ref, bg_ref, lam_ref, y_ref, h_ref):
    w = xr_ref.shape[1]
    x = cw_ref[CONV_WIDTH - 1:CONV_WIDTH, :] * xr_ref[...] + cb_ref[...]
    for j in range(CONV_WIDTH - 1):
        x = x + cw_ref[j:j + 1, :] * buf_ref[:, j * w:(j + 1) * w]
    a, b = _lru_gates(x, wg_ref, bg_ref, lam_ref)
    h = a * h0_ref[...] + b
    h_ref[...] = h
    y_ref[...] = _gelu_tanh(gate_ref[...]) * h


def _lru_sample(xr, buf, gate, h0, conv_w, conv_b, w_gates, b_gates, lam):
    b, w = xr.shape
    return pl.pallas_call(
        _lru_sample_body,
        out_shape=[jax.ShapeDtypeStruct((b, w), F32), jax.ShapeDtypeStruct((b, w), F32)],
        compiler_params=pltpu.CompilerParams(vmem_limit_bytes=VMEM_LIMIT_BYTES),
    )(xr, buf, gate, h0, conv_w, conv_b, w_gates, b_gates, lam)


def _swa_sample_body(q_ref, kn_ref, vn_ref, kc_ref, vc_ref, slope_ref, sink_ref, o_ref, ko_ref, vo_ref):
    bt = q_ref.shape[0]
    lb = kc_ref.shape[1]
    hd = SWA_HEAD_DIM
    head = lax.broadcasted_iota(jnp.int32, (SWA_HEADS, SWA_KV_W), 0)
    col = lax.broadcasted_iota(jnp.int32, (SWA_HEADS, SWA_KV_W), 1)
    own = (head // SWA_GROUP) == (col // hd)
    rel = (lb - lax.broadcasted_iota(jnp.int32, (1, lb), 1)).astype(F32)
    bias = slope_ref[...] * rel
    sink = sink_ref[...]
    for i in range(bt):
        qb = q_ref[i]
        qx = jnp.where(own, jnp.concatenate([qb] * SWA_KV_HEADS, axis=1), 0.0)
        kn = kn_ref[i:i + 1, :]
        vn = vn_ref[i:i + 1, :]
        kc = kc_ref[i]
        vc = vc_ref[i]
        s = _dot_nt(qx, kc) * hd ** -0.5 - bias
        s_new = jnp.sum(qx * kn, axis=-1, keepdims=True) * hd ** -0.5
        m = jnp.maximum(jnp.maximum(jnp.max(s, axis=-1, keepdims=True), s_new), sink)
        p = jnp.exp(s - m)
        p_new = jnp.exp(s_new - m)
        denom = jnp.sum(p, axis=-1, keepdims=True) + p_new + jnp.exp(sink - m)
        ox = jnp.where(own, (_dot(p, vc) + p_new * vn) / denom, 0.0)
        o = ox[:, 0:hd]
        for kh in range(1, SWA_KV_HEADS):
            o = o + ox[:, kh * hd:(kh + 1) * hd]
        o_ref[i] = o
        ko_ref[i, 0:lb - 1, :] = kc_ref[i, 1:lb, :]
        ko_ref[i, lb - 1:lb, :] = kn
        vo_ref[i, 0:lb - 1, :] = vc_ref[i, 1:lb, :]
        vo_ref[i, lb - 1:lb, :] = vn


def _swa_sample(q, kn, vn, kc, vc, slopes, sinks, bt):
    b, lb, _ = kc.shape
    assert b % bt == 0 and lb <= WINDOW
    row = pl.BlockSpec((bt, SWA_KV_W), lambda i: (i, 0))
    cache = pl.BlockSpec((bt, lb, SWA_KV_W), lambda i: (i, 0, 0))
    hq = pl.BlockSpec((bt, SWA_HEADS, SWA_HEAD_DIM), lambda i: (i, 0, 0))
    col = pl.BlockSpec((SWA_HEADS, 1), lambda i: (0, 0))
    return pl.pallas_call(
        _swa_sample_body,
        grid=(b // bt,),
        in_specs=[hq, row, row, cache, cache, col, col],
        out_specs=[hq, cache, cache],
        out_shape=[jax.ShapeDtypeStruct((b, SWA_HEADS, SWA_HEAD_DIM), F32),
                   jax.ShapeDtypeStruct(kc.shape, F32), jax.ShapeDtypeStruct(vc.shape, F32)],
        compiler_params=_params("parallel"),
    )(q, kn, vn, kc, vc, slopes, sinks)


def _block_diag(w):
    n, bi, bj = w.shape
    eye = jnp.eye(n, dtype=w.dtype)
    return (eye[:, None, :, None] * w[:, :, None, :]).reshape(n * bi, n * bj)


def _prepare(prm):
    w_in = prm['w_in_ab'][0]
    c1 = GDN_QKV_W
    c2 = c1 + GDN_V_W
    c3 = c2 + 2 * GDN_HEADS
    lru_w = (w_in.shape[1] - c3) // 2
    d = w_in.shape[0]
    w_in_r = jnp.concatenate(
        [w_in[:, :c2], w_in[:, c3:], w_in[:, c2:c3], jnp.zeros((d, LANES - 2 * GDN_HEADS), w_in.dtype)], axis=1)
    lane_pad = lambda v, at: jnp.zeros((1, LANES), F32).at[0, at:at + v.shape[0]].set(v)
    slopes = jnp.exp2(-8.0 * jnp.arange(1, SWA_HEADS + 1, dtype=F32) / SWA_HEADS)
    return dict(
        w_in=w_in_r.astype(BF16),
        in_splits=(GDN_QKV_W, GDN_V_W, lru_w, lru_w, LANES),
        conv_gdn_w=prm['conv_gdn_w'][0],
        alog_row=lane_pad(prm['gdn_a_log'][0], GDN_HEADS),
        dtb_row=lane_pad(prm['gdn_dt_bias'][0], GDN_HEADS),
        gnorm_w=prm['gdn_norm_w'][0][None, :],
        conv_lru_w=prm['conv_lru_w'][0],
        conv_lru_b=prm['conv_lru_b'][0][None, :],
        w_gates=jnp.concatenate([_block_diag(prm['lru_wa'][0]), _block_diag(prm['lru_wx'][0])], axis=1).astype(BF16),
        b_gates=jnp.concatenate([prm['lru_ba'][0], prm['lru_bx'][0]])[None, :],
        lam=prm['lru_lambda'][0][None, :],
        w_out_gdn=prm['w_out_ab'][0][:GDN_V_W].astype(BF16),
        w_out_lru=prm['w_out_ab'][0][GDN_V_W:].astype(BF16),
        w_qkv_c=prm['w_qkv_c'][0].astype(BF16),
        w_out_c=prm['w_out_c'][0].astype(BF16),
        slopes=slopes,
        sinks=prm['sinks_c'][0],
        w_gate_up=[w.astype(BF16) for w in prm['w_gate_up']],
        w_down=[w.astype(BF16) for w in prm['w_down']],
        norm_mix=[w[None, :] for w in prm['norm_mix']],
        norm_ffn=[w[None, :] for w in prm['norm_ffn']],
        norm_final=prm['norm_final'][None, :],
    )


def _tile(m, pref):
    return pref if m % pref == 0 else m


def _trunk_prompt(x, p):
    b, t, d = x.shape
    m = b * t
    tm = _tile(m, 512)
    tt = _tile(t, 512)
    x0 = x.reshape(m, d)
    qkv, z, gate, xr, ba = _norm_matmul(x0, p['norm_mix'][0], p['w_in'], p['in_splits'], tm)
    r3 = lambda a: a.reshape(b, t, a.shape[-1])
    o_gdn, s_fin = _gdn_prompt(r3(qkv), r3(z), r3(ba), p['conv_gdn_w'], p['alog_row'], p['dtb_row'], p['gnorm_w'], tt)
    y_lru, h_fin = _lru_prompt(r3(xr), r3(gate), p['conv_lru_w'], p['conv_lru_b'], p['w_gates'], p['b_gates'],
                               p['lam'], tt)
    x1 = _proj_residual(x0, [o_gdn.reshape(m, -1), y_lru.reshape(m, -1)], [p['w_out_gdn'], p['w_out_lru']], tm)
    x2 = _ffn(x1, p['norm_ffn'][0], p['w_gate_up'][0], p['w_down'][0], p['norm_final'], False, _tile(m, 1024), 256)
    q, k, v = _norm_matmul(x2, p['norm_mix'][1], p['w_qkv_c'], (SWA_OUT_W, SWA_KV_W, SWA_KV_W), tm)
    att = _swa_prompt(r3(q), r3(k), r3(v), p['slopes'], p['sinks'])
    x3 = _proj_residual(x2, [att.reshape(m, -1)], [p['w_out_c']], tm)
    y = _ffn(x3, p['norm_ffn'][1], p['w_gate_up'][1], p['w_down'][1], p['norm_final'], True, _tile(m, 1024), 256)
    keep = CONV_WIDTH - 1
    kv_tail = lambda a: r3(a)[:, t - WINDOW:].reshape(b, WINDOW, SWA_KV_HEADS, SWA_HEAD_DIM)
    return (y.reshape(b, t, d), s_fin[None], r3(qkv)[:, t - keep:][None], h_fin.reshape(b, -1)[None],
            r3(xr)[:, t - keep:][None], kv_tail(k)[None], kv_tail(v)[None])


def _trunk_sample(x, gdn_s, gdn_cb, lru_h, lru_cb, swa_k, swa_v, p):
    b, t, d = x.shape
    assert t == 1
    x0 = x.reshape(b, d)
    tm = _tile(b, 128)
    bt = _tile(b, 8)
    qkv, z, gate, xr, ba = _norm_matmul(x0, p['norm_mix'][0], p['w_in'], p['in_splits'], tm)
    o_gdn, s_new = _gdn_sample(qkv, gdn_cb[0].reshape(b, -1), z, ba, gdn_s[0], p['conv_gdn_w'], p['alog_row'],
                               p['dtb_row'], p['gnorm_w'], bt)
    y_lru, h_new = _lru_sample(xr, lru_cb[0].reshape(b, -1), gate, lru_h[0], p['conv_lru_w'], p['conv_lru_b'],
                               p['w_gates'], p['b_gates'], p['lam'])
    x1 = _proj_residual(x0, [o_gdn, y_lru], [p['w_out_gdn'], p['w_out_lru']], tm)
    x2 = _ffn(x1, p['norm_ffn'][0], p['w_gate_up'][0], p['w_down'][0], p['norm_final'], False, tm, 256)
    q, k, v = _norm_matmul(x2, p['norm_mix'][1], p['w_qkv_c'], (SWA_OUT_W, SWA_KV_W, SWA_KV_W), tm)
    lb = swa_k.shape[2]
    att, k_new, v_new = _swa_sample(q.reshape(b, SWA_HEADS, SWA_HEAD_DIM), k, v, swa_k[0].reshape(b, lb, SWA_KV_W),
                                    swa_v[0].reshape(b, lb, SWA_KV_W), p['slopes'][:, None], p['sinks'][:, None], bt)
    x3 = _proj_residual(x2, [att.reshape(b, SWA_OUT_W)], [p['w_out_c']], tm)
    y = _ffn(x3, p['norm_ffn'][1], p['w_gate_up'][1], p['w_down'][1], p['norm_final'], True, tm, 256)
    gcb = jnp.concatenate([gdn_cb[0][:, 1:], qkv[:, None, :]], axis=1)
    lcb = jnp.concatenate([lru_cb[0][:, 1:], xr[:, None, :]], axis=1)
    r5 = lambda a: a.reshape(b, lb, SWA_KV_HEADS, SWA_HEAD_DIM)
    return (y.reshape(b, 1, d), s_new[None], gcb[None], h_new[None], lcb[None], r5(k_new)[None], r5(v_new)[None])


def kernel(x_prompt, x_sample, state_gdn, state_gdn_conv, state_lru, state_lru_conv, cache_swa_k, cache_swa_v,
           norm_mix, norm_ffn, norm_final, w_in_ab, conv_gdn_w, gdn_a_log, gdn_dt_bias, gdn_norm_w,
           conv_lru_w, conv_lru_b, lru_wa, lru_ba, lru_wx, lru_bx, lru_lambda, w_out_ab, w_qkv_c, w_out_c,
           sinks_c, w_gate_up, w_down):
    assert norm_mix.shape[0] == 2 and w_in_ab.shape[0] == 1 and w_qkv_c.shape[0] == 1
    p = _prepare(dict(norm_mix=norm_mix, norm_ffn=norm_ffn, norm_final=norm_final, w_in_ab=w_in_ab,
                      conv_gdn_w=conv_gdn_w, gdn_a_log=gdn_a_log, gdn_dt_bias=gdn_dt_bias, gdn_norm_w=gdn_norm_w,
                      conv_lru_w=conv_lru_w, conv_lru_b=conv_lru_b, lru_wa=lru_wa, lru_ba=lru_ba, lru_wx=lru_wx,
                      lru_bx=lru_bx, lru_lambda=lru_lambda, w_out_ab=w_out_ab, w_qkv_c=w_qkv_c, w_out_c=w_out_c,
                      sinks_c=sinks_c, w_gate_up=w_gate_up, w_down=w_down))
    y_p, p_gdn, p_gcb, p_lru, p_lcb, p_k, p_v = _trunk_prompt(x_prompt, p)
    y_s, s_gdn, s_gcb, s_lru, s_lcb, s_k, s_v = _trunk_sample(
        x_sample, state_gdn, state_gdn_conv, state_lru, state_lru_conv, cache_swa_k, cache_swa_v, p)
    return (y_p, y_s, p_gdn, p_gcb, p_lru, p_lcb, p_k, p_v, s_gdn, s_gcb, s_lru, s_lcb, s_k, s_v)
```

```python
import functools
import math

import jax
import jax.numpy as jnp
from jax import lax
from jax.experimental import pallas as pl
from jax.experimental.pallas import tpu as pltpu

F32 = jnp.float32
BF16 = jnp.bfloat16

NORM_EPS = 1e-6
CONV_WIDTH = 4
CONV_TAIL = 8
GDN_HEADS = 4
GDN_DK = 128
GDN_DV = 128
GDN_CHUNK = 64
GDN_QK_W = GDN_HEADS * GDN_DK
GDN_V_W = GDN_HEADS * GDN_DV
GDN_QKV_W = 2 * GDN_QK_W + GDN_V_W
LRU_BLOCKS = 8
LRU_C = 8.0
SWA_HEADS = 16
SWA_KV_HEADS = 4
SWA_GROUP = SWA_HEADS // SWA_KV_HEADS
SWA_HEAD_DIM = 64
WINDOW = 128
SWA_KV_W = SWA_KV_HEADS * SWA_HEAD_DIM
SWA_OUT_W = SWA_HEADS * SWA_HEAD_DIM
LANES = 128
SUBLANES = 8
VMEM_LIMIT_BYTES = 56 * 1024 * 1024


def _params(*semantics):
    return pltpu.CompilerParams(dimension_semantics=semantics, vmem_limit_bytes=VMEM_LIMIT_BYTES)


def _sigmoid(x):
    return 0.5 * jnp.tanh(0.5 * x) + 0.5


def _silu(x):
    return x * _sigmoid(x)


def _softplus(x):
    return jnp.maximum(x, 0.0) + jnp.log1p(jnp.exp(-jnp.abs(x)))


def _gelu_tanh(x):
    return 0.5 * x * (1.0 + jnp.tanh(math.sqrt(2.0 / math.pi) * (x + 0.044715 * (x * x * x))))


def _rms(x, w):
    return x * lax.rsqrt(jnp.mean(x * x, axis=-1, keepdims=True) + NORM_EPS) * w


def _dot(a, b):
    return jnp.dot(a.astype(BF16), b.astype(BF16), preferred_element_type=F32)


def _dot_nt(a, b):
    return lax.dot_general(a.astype(BF16), b.astype(BF16), (((1,), (1,)), ((), ())), preferred_element_type=F32)


def _dot_tn(a, b):
    return lax.dot_general(a.astype(BF16), b.astype(BF16), (((0,), (0,)), ((), ())), preferred_element_type=F32)


def _norm_matmul_body(x_ref, nw_ref, w_ref, *o_refs):
    hb = _rms(x_ref[...], nw_ref[...]).astype(BF16)
    off = 0
    for o_ref in o_refs:
        n = o_ref.shape[-1]
        o_ref[...] = jnp.dot(hb, w_ref[:, off:off + n], preferred_element_type=F32)
        off += n


def _norm_matmul(x, nw, w, splits, tm):
    m, d = x.shape
    n = w.shape[1]
    assert sum(splits) == n and m % tm == 0
    return pl.pallas_call(
        _norm_matmul_body,
        grid=(m // tm,),
        in_specs=[pl.BlockSpec((tm, d), lambda i: (i, 0)),
                  pl.BlockSpec((1, d), lambda i: (0, 0)),
                  pl.BlockSpec((d, n), lambda i: (0, 0))],
        out_specs=[pl.BlockSpec((tm, s), lambda i: (i, 0)) for s in splits],
        out_shape=[jax.ShapeDtypeStruct((m, s), F32) for s in splits],
        compiler_params=_params("parallel"),
        name="norm_matmul",
    )(x, nw, w)


def _proj_residual_body(n_in, res_ref, *refs):
    a_refs, w_refs, o_ref = refs[:n_in], refs[n_in:2 * n_in], refs[2 * n_in]
    acc = res_ref[...]
    for a_ref, w_ref in zip(a_refs, w_refs):
        acc = acc + jnp.dot(a_ref[...].astype(BF16), w_ref[...], preferred_element_type=F32)
    o_ref[...] = acc


def _proj_residual(res, acts, ws, tm):
    m, d = res.shape
    assert m % tm == 0
    return pl.pallas_call(
        functools.partial(_proj_residual_body, len(acts)),
        grid=(m // tm,),
        in_specs=([pl.BlockSpec((tm, d), lambda i: (i, 0))]
                  + [pl.BlockSpec((tm, a.shape[1]), lambda i: (i, 0)) for a in acts]
                  + [pl.BlockSpec(w.shape, lambda i: (0, 0)) for w in ws]),
        out_specs=pl.BlockSpec((tm, d), lambda i: (i, 0)),
        out_shape=jax.ShapeDtypeStruct((m, d), F32),
        compiler_params=_params("parallel"),
        name="proj_residual",
    )(res, *acts, *ws)


def _ffn_body(final_norm, x_ref, nw_ref, wgu_ref, wd_ref, fnw_ref, o_ref):
    d_ff = wd_ref.shape[0]
    x = x_ref[...]
    gu = jnp.dot(_rms(x, nw_ref[...]).astype(BF16), wgu_ref[...], preferred_element_type=F32)
    act = (_silu(gu[:, :d_ff]) * gu[:, d_ff:]).astype(BF16)
    y = x + jnp.dot(act, wd_ref[...], preferred_element_type=F32)
    if final_norm:
        y = _rms(y, fnw_ref[...])
    o_ref[...] = y


def _ffn(x, nw, w_gate_up, w_down, fnw, final_norm, tm):
    m, d = x.shape
    assert m % tm == 0
    resident = lambda a: pl.BlockSpec(a.shape, lambda i: (0,) * a.ndim, pipeline_mode=pl.Buffered(1))
    return pl.pallas_call(
        functools.partial(_ffn_body, final_norm),
        grid=(m // tm,),
        in_specs=[pl.BlockSpec((tm, d), lambda i: (i, 0)),
                  pl.BlockSpec((1, d), lambda i: (0, 0)),
                  resident(w_gate_up), resident(w_down),
                  pl.BlockSpec((1, d), lambda i: (0, 0))],
        out_specs=pl.BlockSpec((tm, d), lambda i: (i, 0)),
        out_shape=jax.ShapeDtypeStruct((m, d), F32),
        compiler_params=_params("parallel"),
        name="ffn",
    )(x, nw, w_gate_up, w_down, fnw)


def _conv_tile(x, w_ref, tail_scr, first_tile):
    tt = x.shape[0]

    @pl.when(first_tile)
    def _():
        tail_scr[...] = jnp.zeros_like(tail_scr)

    tail = tail_scr[...]
    row = lax.broadcasted_iota(jnp.int32, tail.shape, 0)
    y = w_ref[CONV_WIDTH - 1:CONV_WIDTH, :] * x
    for s in range(1, CONV_WIDTH):
        xs = pltpu.roll(x, s, axis=0)
        head = jnp.where(row < s, pltpu.roll(tail, s, axis=0), xs[0:CONV_TAIL])
        y = y + w_ref[CONV_WIDTH - 1 - s:CONV_WIDTH - s, :] * jnp.concatenate([head, xs[CONV_TAIL:]], axis=0)
    tail_scr[...] = x[tt - CONV_TAIL:tt]
    return y


def _bmm(a, b):
    return lax.dot_general(a.astype(BF16), b.astype(BF16), (((2,), (1,)), ((0,), (0,))), preferred_element_type=F32)


def _bmm_nt(a, b):
    return lax.dot_general(a.astype(BF16), b.astype(BF16), (((2,), (2,)), ((0,), (0,))), preferred_element_type=F32)


def _gdn_prompt_body(qkv_ref, z_ref, ba_ref, cw_ref, alog_ref, dtb_ref, gnw_ref, o_ref, s_ref,
                     xp_scr, wkqd_scr, wv_scr, qk_scr, ktt_scr, gtot_scr):
    t = pl.program_id(1)
    tt = qkv_ref.shape[1]
    c = GDN_CHUNK
    nc = tt // c
    h_n = GDN_HEADS
    nb = h_n * nc

    @pl.when(t == 0)
    def _():
        s_ref[...] = jnp.zeros_like(s_ref)

    act = _silu(_conv_tile(qkv_ref[0], cw_ref, xp_scr, t == 0))

    def heads(off):
        return jnp.stack([act[:, off + h * GDN_DK:off + (h + 1) * GDN_DK] for h in range(h_n)])

    q = heads(0)
    q = q * lax.rsqrt(jnp.sum(q * q, axis=-1, keepdims=True) + NORM_EPS) * GDN_DK ** -0.5
    k = heads(GDN_QK_W)
    k = k * lax.rsqrt(jnp.sum(k * k, axis=-1, keepdims=True) + NORM_EPS)
    v = heads(2 * GDN_QK_W)
    q, k, v = (a.reshape(nb, c, GDN_DK) for a in (q, k, v))

    ba = ba_ref[0]
    beta = _sigmoid(ba)
    g = -jnp.exp(alog_ref[...]) * _softplus(ba + dtb_ref[...])
    row = lax.broadcasted_iota(jnp.int32, ba.shape, 0) % c
    gam = g
    shift = 1
    while shift < c:
        gam = gam + jnp.where(row >= shift, pltpu.roll(gam, shift, axis=0), 0.0)
        shift *= 2
    gam_t = gam.T

    def col(a, first):
        return jnp.stack([a[:, first + h:first + h + 1] for h in range(h_n)]).reshape(nb, c, 1)

    beta_c = col(beta, 0)
    gam_c = col(gam, h_n)
    gam_r = jnp.stack([gam_t[h_n + h:h_n + h + 1, ci * c:(ci + 1) * c]
                       for h in range(h_n) for ci in range(nc)])
    gam_last = gam_c[:, c - 1:c, :]

    ri = lax.broadcasted_iota(jnp.int32, (1, c, c), 1)
    cj = lax.broadcasted_iota(jnp.int32, (1, c, c), 2)
    decay = jnp.exp(jnp.where(ri >= cj, gam_c - gam_r, -jnp.inf))
    kk = _bmm_nt(k, k)
    qk = _bmm_nt(q, k) * decay
    pw = jnp.where(ri > cj, -(beta_c * kk * decay), 0.0)
    tq = pw
    for _ in range(int(math.log2(c)) - 1):
        pw = _bmm(pw, pw)
        tq = tq + pw + _bmm(tq, pw)
    e_gam = jnp.exp(gam_c)
    rhs = jnp.concatenate([beta_c * v, (beta_c * e_gam) * k], axis=2)
    sol = rhs + _bmm(tq, rhs)
    w_key = sol[:, :, GDN_DV:]
    k_tail = k * jnp.exp(gam_last - gam_c)
    grp = lambda a: a.reshape((h_n, nc) + a.shape[1:])
    wv_scr[...] = grp(sol[:, :, :GDN_DV])
    wkqd_scr[...] = grp(jnp.concatenate([w_key, q * e_gam], axis=1).astype(BF16))
    qk_scr[...] = grp(qk.astype(BF16))
    ktt_scr[...] = grp(jnp.swapaxes(k_tail, 1, 2).astype(BF16))
    gtot_scr[...] = grp(jnp.broadcast_to(jnp.exp(gam_last), (nb, 1, GDN_DV)))

    def chunk(ci, carry):
        r0 = pl.multiple_of(ci * c, c)
        s_old = s_ref[0]
        r = _bmm(wkqd_scr[:, ci], s_old)
        u = wv_scr[:, ci] - r[:, :c]
        o = r[:, c:] + _bmm(qk_scr[:, ci], u)
        s_ref[0] = s_old * gtot_scr[:, ci] + _bmm(ktt_scr[:, ci], u)
        o = o * lax.rsqrt(jnp.mean(o * o, axis=-1, keepdims=True) + NORM_EPS) * gnw_ref[...]
        for h in range(h_n):
            lo, hi = h * GDN_DV, (h + 1) * GDN_DV
            o_ref[0, pl.ds(r0, c), lo:hi] = o[h] * _silu(z_ref[0, pl.ds(r0, c), lo:hi])
        return carry

    lax.fori_loop(0, nc, chunk, 0)


def _gdn_prompt(qkv, z, ba, conv_w, alog_row, dtb_row, gnorm_w, tt):
    b, t, _ = qkv.shape
    assert t % tt == 0 and tt % GDN_CHUNK == 0
    c, hn, nc = GDN_CHUNK, GDN_HEADS, tt // GDN_CHUNK
    row_spec = lambda w: pl.BlockSpec((1, tt, w), lambda i, j: (i, j, 0))
    full = lambda a: pl.BlockSpec(a.shape, lambda i, j: (0,) * a.ndim)
    return pl.pallas_call(
        _gdn_prompt_body,
        grid=(b, t // tt),
        in_specs=[row_spec(GDN_QKV_W), row_spec(GDN_V_W), row_spec(LANES),
                  full(conv_w), full(alog_row), full(dtb_row), full(gnorm_w)],
        out_specs=[row_spec(GDN_V_W),
                   pl.BlockSpec((1, hn, GDN_DK, GDN_DV), lambda i, j: (i, 0, 0, 0))],
        out_shape=[jax.ShapeDtypeStruct((b, t, GDN_V_W), F32),
                   jax.ShapeDtypeStruct((b, hn, GDN_DK, GDN_DV), F32)],
        scratch_shapes=[pltpu.VMEM((CONV_TAIL, GDN_QKV_W), F32),
                        pltpu.VMEM((hn, nc, 2 * c, GDN_DK), BF16),
                        pltpu.VMEM((hn, nc, c, GDN_DV), F32),
                        pltpu.VMEM((hn, nc, c, c), BF16),
                        pltpu.VMEM((hn, nc, GDN_DK, c), BF16),
                        pltpu.VMEM((hn, nc, 1, GDN_DV), F32)],
        compiler_params=_params("parallel", "arbitrary"),
        name="gdn_prompt",
    )(qkv, z, ba, conv_w, alog_row, dtb_row, gnorm_w)


def _lru_gates(x, wg_ref, bg_ref, lam_ref):
    w = x.shape[1]
    gates = jnp.dot(x.astype(BF16), wg_ref[...], preferred_element_type=F32) + bg_ref[...]
    r = _sigmoid(gates[:, :w])
    i = _sigmoid(gates[:, w:])
    log_a = -LRU_C * r * _softplus(-lam_ref[...])
    a = jnp.exp(log_a)
    b = jnp.sqrt(jnp.tanh(-log_a) * (1.0 + a * a)) * (i * x)
    return a, b


def _lru_prompt_body(xr_ref, gate_ref, cw_ref, cb_ref, wg_ref, bg_ref, lam_ref, y_ref, h_ref, xp_scr):
    t = pl.program_id(1)
    tt = xr_ref.shape[1]

    @pl.when(t == 0)
    def _():
        h_ref[...] = jnp.zeros_like(h_ref)

    x = _conv_tile(xr_ref[0], cw_ref, xp_scr, t == 0) + cb_ref[...]
    a, b = _lru_gates(x, wg_ref, bg_ref, lam_ref)
    w = a.shape[1]
    ng = tt // SUBLANES
    a = a.reshape(ng, SUBLANES, w)
    b = b.reshape(ng, SUBLANES, w)
    sub = lax.broadcasted_iota(jnp.int32, (1, SUBLANES, w), 1)
    shift = 1
    while shift < SUBLANES:
        keep = sub >= shift
        a_prev = jnp.where(keep, pltpu.roll(a, shift, axis=1), 1.0)
        b_prev = jnp.where(keep, pltpu.roll(b, shift, axis=1), 0.0)
        b = a * b_prev + b
        a = a * a_prev
        shift *= 2
    h = h_ref[0]
    groups = []
    for i in range(ng):
        hs_i = b[i] + a[i] * h
        groups.append(hs_i)
        h = hs_i[SUBLANES - 1:SUBLANES, :]
    h_ref[0] = h
    y_ref[0] = _gelu_tanh(gate_ref[0]) * jnp.concatenate(groups, axis=0)


def _lru_prompt(xr, gate, conv_w, conv_b, w_gates, b_gates, lam, tt):
    b, t, w = xr.shape
    assert t % tt == 0
    row_spec = pl.BlockSpec((1, tt, w), lambda i, j: (i, j, 0))
    full = lambda a: pl.BlockSpec(a.shape, lambda i, j: (0,) * a.ndim)
    return pl.pallas_call(
        _lru_prompt_body,
        grid=(b, t // tt),
        in_specs=[row_spec, row_spec, full(conv_w), full(conv_b), full(w_gates), full(b_gates), full(lam)],
        out_specs=[row_spec, pl.BlockSpec((1, 1, w), lambda i, j: (i, 0, 0))],
        out_shape=[jax.ShapeDtypeStruct((b, t, w), F32), jax.ShapeDtypeStruct((b, 1, w), F32)],
        scratch_shapes=[pltpu.VMEM((CONV_TAIL, w), F32)],
        compiler_params=_params("parallel", "arbitrary"),
        name="lru_prompt",
    )(xr, gate, conv_w, conv_b, w_gates, b_gates, lam)


def _swa_prompt_body(slope_ref, sink_ref, q_ref, kp_ref, kc_ref, vp_ref, vc_ref, o_ref, bias_scr):
    n = pl.program_id(1)
    w = WINDOW
    hd = SWA_HEAD_DIM

    @pl.when((pl.program_id(0) == 0) & (n == 0))
    def _():
        qi = lax.broadcasted_iota(jnp.int32, (w, 2 * w), 0)
        kj = lax.broadcasted_iota(jnp.int32, (w, 2 * w), 1)
        rel = qi + w - kj
        band = (rel >= 0) & (rel <= w)
        relf = rel.astype(F32)
        for hh in range(SWA_HEADS):
            bias = -(slope_ref[hh] * relf)
            bias_scr[0, hh] = jnp.where(band, bias, -jnp.inf)
            bias_scr[1, hh] = jnp.where(band & (kj >= w), bias, -jnp.inf)

    first = (n == 0).astype(jnp.int32)
    for kh in range(SWA_KV_HEADS):
        kcat = jnp.concatenate([kp_ref[0, :, kh * hd:(kh + 1) * hd], kc_ref[0, :, kh * hd:(kh + 1) * hd]], axis=0)
        vcat = jnp.concatenate([vp_ref[0, :, kh * hd:(kh + 1) * hd], vc_ref[0, :, kh * hd:(kh + 1) * hd]], axis=0)
        for gi in range(SWA_GROUP):
            hh = kh * SWA_GROUP + gi
            s = _dot_nt(q_ref[0, :, hh * hd:(hh + 1) * hd] * hd ** -0.5, kcat) + bias_scr[first, hh]
            sink = sink_ref[hh]
            m = jnp.maximum(jnp.max(s, axis=-1, keepdims=True), sink)
            p = jnp.exp(s - m)
            denom = jnp.sum(p, axis=-1, keepdims=True) + jnp.exp(sink - m)
            o_ref[0, :, hh * hd:(hh + 1) * hd] = _dot(p, vcat) / denom


def _swa_prompt(q, k, v, slopes, sinks):
    b, t, _ = q.shape
    nb = t // WINDOW
    smem = pl.BlockSpec(memory_space=pltpu.SMEM)
    prev = pl.BlockSpec((1, WINDOW, SWA_KV_W), lambda i, j: (i, jnp.maximum(j - 1, 0), 0))
    cur = pl.BlockSpec((1, WINDOW, SWA_KV_W), lambda i, j: (i, j, 0))
    return pl.pallas_call(
        _swa_prompt_body,
        grid=(b, nb),
        in_specs=[smem, smem, pl.BlockSpec((1, WINDOW, SWA_OUT_W), lambda i, j: (i, j, 0)), prev, cur, prev, cur],
        out_specs=pl.BlockSpec((1, WINDOW, SWA_OUT_W), lambda i, j: (i, j, 0)),
        out_shape=jax.ShapeDtypeStruct((b, t, SWA_OUT_W), F32),
        scratch_shapes=[pltpu.VMEM((2, SWA_HEADS, WINDOW, 2 * WINDOW), F32)],
        compiler_params=_params("arbitrary", "arbitrary"),
        name="swa_prompt",
    )(slopes, sinks, q, k, k, v, v)


def _gdn_sample_body(qkv_ref, buf_ref, z_ref, ba_ref, s_in_ref, cw_ref, alog_ref, dtb_ref, gnw_ref,
                     o_ref, s_out_ref, o_scr):
    bt = qkv_ref.shape[0]
    wq = GDN_QKV_W
    y = cw_ref[CONV_WIDTH - 1:CONV_WIDTH, :] * qkv_ref[...]
    for j in range(CONV_WIDTH - 1):
        y = y + cw_ref[j:j + 1, :] * buf_ref[:, j * wq:(j + 1) * wq]
    act = _silu(y)
    ba = ba_ref[...]
    beta = _sigmoid(ba)
    eg = jnp.exp(-jnp.exp(alog_ref[...]) * _softplus(ba + dtb_ref[...]))
    pad = jnp.zeros((LANES - bt, GDN_DK), F32)
    for h in range(GDN_HEADS):
        lo, hi = h * GDN_DK, (h + 1) * GDN_DK
        qh = act[:, lo:hi]
        qh = qh * lax.rsqrt(jnp.sum(qh * qh, axis=-1, keepdims=True) + NORM_EPS) * GDN_DK ** -0.5
        kh = act[:, GDN_QK_W + lo:GDN_QK_W + hi]
        kh = kh * lax.rsqrt(jnp.sum(kh * kh, axis=-1, keepdims=True) + NORM_EPS)
        vh = act[:, 2 * GDN_QK_W + lo:2 * GDN_QK_W + hi]
        qt = jnp.concatenate([qh, pad], axis=0).T
        kt = jnp.concatenate([kh, pad], axis=0).T
        for i in range(bt):
            kcol = kt[:, i:i + 1]
            qcol = qt[:, i:i + 1]
            s_dec = s_in_ref[i, h] * eg[i:i + 1, GDN_HEADS + h:GDN_HEADS + h + 1]
            pred = jnp.sum(kcol * s_dec, axis=0, keepdims=True)
            upd = beta[i:i + 1, h:h + 1] * (vh[i:i + 1, :] - pred)
            s_new = s_dec + kcol * upd
            s_out_ref[i, h] = s_new
            o_scr[i:i + 1, lo:hi] = jnp.sum(qcol * s_new, axis=0, keepdims=True)
    for h in range(GDN_HEADS):
        lo, hi = h * GDN_DV, (h + 1) * GDN_DV
        o = o_scr[:, lo:hi]
        o = o * lax.rsqrt(jnp.mean(o * o, axis=-1, keepdims=True) + NORM_EPS) * gnw_ref[...]
        o_ref[:, lo:hi] = o * _silu(z_ref[:, lo:hi])


def _gdn_sample(qkv, buf, z, ba, state, conv_w, alog_row, dtb_row, gnorm_w, bt):
    b = qkv.shape[0]
    assert b % bt == 0
    row = lambda a: pl.BlockSpec((bt, a.shape[1]), lambda i: (i, 0))
    full = lambda a: pl.BlockSpec(a.shape, lambda i: (0,) * a.ndim)
    st = pl.BlockSpec((bt, GDN_HEADS, GDN_DK, GDN_DV), lambda i: (i, 0, 0, 0))
    return pl.pallas_call(
        _gdn_sample_body,
        grid=(b // bt,),
        in_specs=[row(qkv), row(buf), row(z), row(ba), st, full(conv_w), full(alog_row), full(dtb_row), full(gnorm_w)],
        out_specs=[pl.BlockSpec((bt, GDN_V_W), lambda i: (i, 0)), st],
        out_shape=[jax.ShapeDtypeStruct((b, GDN_V_W), F32), jax.ShapeDtypeStruct(state.shape, F32)],
        scratch_shapes=[pltpu.VMEM((bt, GDN_V_W), F32)],
        compiler_params=_params("parallel"),
        name="gdn_sample",
    )(qkv, buf, z, ba, state, conv_w, alog_row, dtb_row, gnorm_w)


def _lru_sample_body(xr_ref, buf_ref, gate_ref, h0_ref, cw_ref, cb_ref, wg_ref, bg_ref, lam_ref, y_ref, h_ref):
    w = xr_ref.shape[1]
    x = cw_ref[CONV_WIDTH - 1:CONV_WIDTH, :] * xr_ref[...] + cb_ref[...]
    for j in range(CONV_WIDTH - 1):
        x = x + cw_ref[j:j + 1, :] * buf_ref[:, j * w:(j + 1) * w]
    a, b = _lru_gates(x, wg_ref, bg_ref, lam_ref)
    h = a * h0_ref[...] + b
    h_ref[...] = h
    y_ref[...] = _gelu_tanh(gate_ref[...]) * h


def _lru_sample(xr, buf, gate, h0, conv_w, conv_b, w_gates, b_gates, lam):
    b, w = xr.shape
    return pl.pallas_call(
        _lru_sample_body,
        out_shape=[jax.ShapeDtypeStruct((b, w), F32), jax.ShapeDtypeStruct((b, w), F32)],
        compiler_params=pltpu.CompilerParams(vmem_limit_bytes=VMEM_LIMIT_BYTES),
        name="lru_sample",
    )(xr, buf, gate, h0, conv_w, conv_b, w_gates, b_gates, lam)


def _swa_sample_body(q_ref, kn_ref, vn_ref, kc_ref, vc_ref, slope_ref, sink_ref, o_ref, ko_ref, vo_ref):
    bt = q_ref.shape[0]
    lb = kc_ref.shape[1]
    hd = SWA_HEAD_DIM
    head = lax.broadcasted_iota(jnp.int32, (SWA_HEADS, SWA_KV_W), 0)
    col = lax.broadcasted_iota(jnp.int32, (SWA_HEADS, SWA_KV_W), 1)
    own = (head // SWA_GROUP) == (col // hd)
    rel = (lb - lax.broadcasted_iota(jnp.int32, (1, lb), 1)).astype(F32)
    bias = slope_ref[...] * rel
    sink = sink_ref[...]
    for i in range(bt):
        qb = q_ref[i]
        qx = jnp.where(own, jnp.concatenate([qb] * SWA_KV_HEADS, axis=1), 0.0)
        kn = kn_ref[i:i + 1, :]
        vn = vn_ref[i:i + 1, :]
        kc = kc_ref[i]
        vc = vc_ref[i]
        s = _dot_nt(qx, kc) * hd ** -0.5 - bias
        s_new = jnp.sum(qx * kn, axis=-1, keepdims=True) * hd ** -0.5
        m = jnp.maximum(jnp.maximum(jnp.max(s, axis=-1, keepdims=True), s_new), sink)
        p = jnp.exp(s - m)
        p_new = jnp.exp(s_new - m)
        denom = jnp.sum(p, axis=-1, keepdims=True) + p_new + jnp.exp(sink - m)
        ox = jnp.where(own, (_dot(p, vc) + p_new * vn) / denom, 0.0)
        o = ox[:, 0:hd]
        for kh in range(1, SWA_KV_HEADS):
            o = o + ox[:, kh * hd:(kh + 1) * hd]
        o_ref[i] = o
        ko_ref[i, 0:lb - 1, :] = kc_ref[i, 1:lb, :]
        ko_ref[i, lb - 1:lb, :] = kn
        vo_ref[i, 0:lb - 1, :] = vc_ref[i, 1:lb, :]
        vo_ref[i, lb - 1:lb, :] = vn


def _swa_sample(q, kn, vn, kc, vc, slopes, sinks, bt):
    b, lb, _ = kc.shape
    assert b % bt == 0 and lb <= WINDOW
    row = pl.BlockSpec((bt, SWA_KV_W), lambda i: (i, 0))
    cache = pl.BlockSpec((bt, lb, SWA_KV_W), lambda i: (i, 0, 0))
    hq = pl.BlockSpec((bt, SWA_HEADS, SWA_HEAD_DIM), lambda i: (i, 0, 0))
    col = pl.BlockSpec((SWA_HEADS, 1), lambda i: (0, 0))
    return pl.pallas_call(
        _swa_sample_body,
        grid=(b // bt,),
        in_specs=[hq, row, row, cache, cache, col, col],
        out_specs=[hq, cache, cache],
        out_shape=[jax.ShapeDtypeStruct((b, SWA_HEADS, SWA_HEAD_DIM), F32),
                   jax.ShapeDtypeStruct(kc.shape, F32), jax.ShapeDtypeStruct(vc.shape, F32)],
        compiler_params=_params("parallel"),
        name="swa_sample",
    )(q, kn, vn, kc, vc, slopes, sinks)


def _block_diag(w):
    n, bi, bj = w.shape
    eye = jnp.eye(n, dtype=w.dtype)
    return (eye[:, None, :, None] * w[:, :, None, :]).reshape(n * bi, n * bj)


def _prepare(prm):
    w_in = prm['w_in_ab'][0]
    c1 = GDN_QKV_W
    c2 = c1 + GDN_V_W
    c3 = c2 + 2 * GDN_HEADS
    lru_w = (w_in.shape[1] - c3) // 2
    d = w_in.shape[0]
    w_in_r = jnp.concatenate(
        [w_in[:, :c2], w_in[:, c3:], w_in[:, c2:c3], jnp.zeros((d, LANES - 2 * GDN_HEADS), w_in.dtype)], axis=1)
    lane_pad = lambda v, at: jnp.zeros((1, LANES), F32).at[0, at:at + v.shape[0]].set(v)
    slopes = jnp.exp2(-8.0 * jnp.arange(1, SWA_HEADS + 1, dtype=F32) / SWA_HEADS)
    return dict(
        w_in=w_in_r.astype(BF16),
        in_splits=(GDN_QKV_W, GDN_V_W, lru_w, lru_w, LANES),
        conv_gdn_w=prm['conv_gdn_w'][0],
        alog_row=lane_pad(prm['gdn_a_log'][0], GDN_HEADS),
        dtb_row=lane_pad(prm['gdn_dt_bias'][0], GDN_HEADS),
        gnorm_w=prm['gdn_norm_w'][0][None, :],
        conv_lru_w=prm['conv_lru_w'][0],
        conv_lru_b=prm['conv_lru_b'][0][None, :],
        w_gates=jnp.concatenate([_block_diag(prm['lru_wa'][0]), _block_diag(prm['lru_wx'][0])], axis=1).astype(BF16),
        b_gates=jnp.concatenate([prm['lru_ba'][0], prm['lru_bx'][0]])[None, :],
        lam=prm['lru_lambda'][0][None, :],
        w_out_gdn=prm['w_out_ab'][0][:GDN_V_W].astype(BF16),
        w_out_lru=prm['w_out_ab'][0][GDN_V_W:].astype(BF16),
        w_qkv_c=prm['w_qkv_c'][0].astype(BF16),
        w_out_c=prm['w_out_c'][0].astype(BF16),
        slopes=slopes,
        sinks=prm['sinks_c'][0],
        w_gate_up=[w.astype(BF16) for w in prm['w_gate_up']],
        w_down=[w.astype(BF16) for w in prm['w_down']],
        norm_mix=[w[None, :] for w in prm['norm_mix']],
        norm_ffn=[w[None, :] for w in prm['norm_ffn']],
        norm_final=prm['norm_final'][None, :],
    )


def _tile(m, pref):
    return pref if m % pref == 0 else m


def _trunk_prompt(x, p):
    b, t, d = x.shape
    m = b * t
    tm = _tile(m, 512)
    tt = _tile(t, 512)
    x0 = x.reshape(m, d)
    qkv, z, gate, xr, ba = _norm_matmul(x0, p['norm_mix'][0], p['w_in'], p['in_splits'], tm)
    r3 = lambda a: a.reshape(b, t, a.shape[-1])
    o_gdn, s_fin = _gdn_prompt(r3(qkv), r3(z), r3(ba), p['conv_gdn_w'], p['alog_row'], p['dtb_row'], p['gnorm_w'], tt)
    y_lru, h_fin = _lru_prompt(r3(xr), r3(gate), p['conv_lru_w'], p['conv_lru_b'], p['w_gates'], p['b_gates'],
                               p['lam'], tt)
    x1 = _proj_residual(x0, [o_gdn.reshape(m, -1), y_lru.reshape(m, -1)], [p['w_out_gdn'], p['w_out_lru']], tm)
    x2 = _ffn(x1, p['norm_ffn'][0], p['w_gate_up'][0], p['w_down'][0], p['norm_final'], False, tm)
    q, k, v = _norm_matmul(x2, p['norm_mix'][1], p['w_qkv_c'], (SWA_OUT_W, SWA_KV_W, SWA_KV_W), tm)
    att = _swa_prompt(r3(q), r3(k), r3(v), p['slopes'], p['sinks'])
    x3 = _proj_residual(x2, [att.reshape(m, -1)], [p['w_out_c']], tm)
    y = _ffn(x3, p['norm_ffn'][1], p['w_gate_up'][1], p['w_down'][1], p['norm_final'], True, tm)
    keep = CONV_WIDTH - 1
    kv_tail = lambda a: r3(a)[:, t - WINDOW:].reshape(b, WINDOW, SWA_KV_HEADS, SWA_HEAD_DIM)
    return (y.reshape(b, t, d), s_fin[None], r3(qkv)[:, t - keep:][None], h_fin.reshape(b, -1)[None],
            r3(xr)[:, t - keep:][None], kv_tail(k)[None], kv_tail(v)[None])


def _trunk_sample(x, gdn_s, gdn_cb, lru_h, lru_cb, swa_k, swa_v, p):
    b, t, d = x.shape
    assert t == 1
    x0 = x.reshape(b, d)
    tm = _tile(b, 128)
    bt = _tile(b, 8)
    qkv, z, gate, xr, ba = _norm_matmul(x0, p['norm_mix'][0], p['w_in'], p['in_splits'], tm)
    o_gdn, s_new = _gdn_sample(qkv, gdn_cb[0].reshape(b, -1), z, ba, gdn_s[0], p['conv_gdn_w'], p['alog_row'],
                               p['dtb_row'], p['gnorm_w'], bt)
    y_lru, h_new = _lru_sample(xr, lru_cb[0].reshape(b, -1), gate, lru_h[0], p['conv_lru_w'], p['conv_lru_b'],
                               p['w_gates'], p['b_gates'], p['lam'])
    x1 = _proj_residual(x0, [o_gdn, y_lru], [p['w_out_gdn'], p['w_out_lru']], tm)
    x2 = _ffn(x1, p['norm_ffn'][0], p['w_gate_up'][0], p['w_down'][0], p['norm_final'], False, tm)
    q, k, v = _norm_matmul(x2, p['norm_mix'][1], p['w_qkv_c'], (SWA_OUT_W, SWA_KV_W, SWA_KV_W), tm)
    lb = swa_k.shape[2]
    att, k_new, v_new = _swa_sample(q.reshape(b, SWA_HEADS, SWA_HEAD_DIM), k, v, swa_k[0].reshape(b, lb, SWA_KV_W),
                                    swa_v[0].reshape(b, lb, SWA_KV_W), p['slopes'][:, None], p['sinks'][:, None], bt)
    x3 = _proj_residual(x2, [att.reshape(b, SWA_OUT_W)], [p['w_out_c']], tm)
    y = _ffn(x3, p['norm_ffn'][1], p['w_gate_up'][1], p['w_down'][1], p['norm_final'], True, tm)
    gcb = jnp.concatenate([gdn_cb[0][:, 1:], qkv[:, None, :]], axis=1)
    lcb = jnp.concatenate([lru_cb[0][:, 1:], xr[:, None, :]], axis=1)
    r5 = lambda a: a.reshape(b, lb, SWA_KV_HEADS, SWA_HEAD_DIM)
    return (y.reshape(b, 1, d), s_new[None], gcb[None], h_new[None], lcb[None], r5(k_new)[None], r5(v_new)[None])


def kernel(x_prompt, x_sample, state_gdn, state_gdn_conv, state_lru, state_lru_conv, cache_swa_k, cache_swa_v,
           norm_mix, norm_ffn, norm_final, w_in_ab, conv_gdn_w, gdn_a_log, gdn_dt_bias, gdn_norm_w,
           conv_lru_w, conv_lru_b, lru_wa, lru_ba, lru_wx, lru_bx, lru_lambda, w_out_ab, w_qkv_c, w_out_c,
           sinks_c, w_gate_up, w_down):
    assert norm_mix.shape[0] == 2 and w_in_ab.shape[0] == 1 and w_qkv_c.shape[0] == 1
    p = _prepare(dict(norm_mix=norm_mix, norm_ffn=norm_ffn, norm_final=norm_final, w_in_ab=w_in_ab,
                      conv_gdn_w=conv_gdn_w, gdn_a_log=gdn_a_log, gdn_dt_bias=gdn_dt_bias, gdn_norm_w=gdn_norm_w,
                      conv_lru_w=conv_lru_w, conv_lru_b=conv_lru_b, lru_wa=lru_wa, lru_ba=lru_ba, lru_wx=lru_wx,
                      lru_bx=lru_bx, lru_lambda=lru_lambda, w_out_ab=w_out_ab, w_qkv_c=w_qkv_c, w_out_c=w_out_c,
                      sinks_c=sinks_c, w_gate_up=w_gate_up, w_down=w_down))
    y_p, p_gdn, p_gcb, p_lru, p_lcb, p_k, p_v = _trunk_prompt(x_prompt, p)
    y_s, s_gdn, s_gcb, s_lru, s_lcb, s_k, s_v = _trunk_sample(
        x_sample, state_gdn, state_gdn_conv, state_lru, state_lru_conv, cache_swa_k, cache_swa_v, p)
    return (y_p, y_s, p_gdn, p_gcb, p_lru, p_lcb, p_k, p_v, s_gdn, s_gcb, s_lru, s_lcb, s_k, s_v)
```

```python
import functools
import math

import jax
import jax.numpy as jnp
from jax import lax
from jax.experimental import pallas as pl
from jax.experimental.pallas import tpu as pltpu

F32 = jnp.float32
BF16 = jnp.bfloat16

NORM_EPS = 1e-6
CONV_WIDTH = 4
CONV_TAIL = 8
GDN_HEADS = 4
GDN_DK = 128
GDN_DV = 128
GDN_CHUNK = 64
GDN_QK_W = GDN_HEADS * GDN_DK
GDN_V_W = GDN_HEADS * GDN_DV
GDN_QKV_W = 2 * GDN_QK_W + GDN_V_W
LRU_BLOCKS = 8
LRU_C = 8.0
SWA_HEADS = 16
SWA_KV_HEADS = 4
SWA_GROUP = SWA_HEADS // SWA_KV_HEADS
SWA_HEAD_DIM = 64
WINDOW = 128
SWA_KV_W = SWA_KV_HEADS * SWA_HEAD_DIM
SWA_OUT_W = SWA_HEADS * SWA_HEAD_DIM
SWA_SPLITS = (SWA_OUT_W, SWA_KV_W, SWA_KV_W)
SWA_DTYPES = (BF16, F32, F32)
LANES = 128
SUBLANES = 8
VMEM_LIMIT_BYTES = 56 * 1024 * 1024


def _params(*semantics):
    return pltpu.CompilerParams(dimension_semantics=semantics, vmem_limit_bytes=VMEM_LIMIT_BYTES)


def _sigmoid(x):
    return 0.5 * jnp.tanh(0.5 * x) + 0.5


def _silu(x):
    return x * _sigmoid(x)


def _softplus(x):
    return jnp.maximum(x, 0.0) + jnp.log1p(jnp.exp(-jnp.abs(x)))


def _gelu_tanh(x):
    return 0.5 * x * (1.0 + jnp.tanh(math.sqrt(2.0 / math.pi) * (x + 0.044715 * (x * x * x))))


def _rms(x, w):
    return x * lax.rsqrt(jnp.mean(x * x, axis=-1, keepdims=True) + NORM_EPS) * w


def _dot(a, b):
    return jnp.dot(a.astype(BF16), b.astype(BF16), preferred_element_type=F32)


def _dot_nt(a, b):
    return lax.dot_general(a.astype(BF16), b.astype(BF16), (((1,), (1,)), ((), ())), preferred_element_type=F32)


def _dot_tn(a, b):
    return lax.dot_general(a.astype(BF16), b.astype(BF16), (((0,), (0,)), ((), ())), preferred_element_type=F32)


def _norm_matmul_body(x_ref, nw_ref, w_ref, *o_refs):
    hb = _rms(x_ref[...], nw_ref[...]).astype(BF16)
    off = 0
    for o_ref in o_refs:
        n = o_ref.shape[-1]
        o_ref[...] = jnp.dot(hb, w_ref[:, off:off + n], preferred_element_type=F32)
        off += n


def _norm_matmul(x, nw, w, splits, tm):
    m, d = x.shape
    n = w.shape[1]
    assert sum(splits) == n and m % tm == 0
    return pl.pallas_call(
        _norm_matmul_body,
        grid=(m // tm,),
        in_specs=[pl.BlockSpec((tm, d), lambda i: (i, 0)),
                  pl.BlockSpec((1, d), lambda i: (0, 0)),
                  pl.BlockSpec((d, n), lambda i: (0, 0))],
        out_specs=[pl.BlockSpec((tm, s), lambda i: (i, 0)) for s in splits],
        out_shape=[jax.ShapeDtypeStruct((m, s), F32) for s in splits],
        compiler_params=_params("parallel"),
        name="norm_matmul",
    )(x, nw, w)


FFN_CHUNKS = 1


def _layer_tail_body(n_act, next_splits, res_ref, *refs):
    a_refs, wo_refs = refs[:n_act], refs[n_act:2 * n_act]
    nw_ref, wgu_ref, wd_ref, nnw_ref = refs[2 * n_act:2 * n_act + 4]
    rest = refs[2 * n_act + 4:]
    x = res_ref[...]
    for a_ref, wo_ref in zip(a_refs, wo_refs):
        x = x + jnp.dot(a_ref[...].astype(BF16), wo_ref[...], preferred_element_type=F32)
    hb = _rms(x, nw_ref[...]).astype(BF16)
    d_ff = wd_ref.shape[0]
    cf = d_ff // FFN_CHUNKS
    y = x
    for c in range(FFN_CHUNKS):
        g = jnp.dot(hb, wgu_ref[:, c * cf:(c + 1) * cf], preferred_element_type=F32)
        u = jnp.dot(hb, wgu_ref[:, d_ff + c * cf:d_ff + (c + 1) * cf], preferred_element_type=F32)
        y = y + jnp.dot((_silu(g) * u).astype(BF16), wd_ref[c * cf:(c + 1) * cf, :], preferred_element_type=F32)
    if not next_splits:
        rest[0][...] = _rms(y, nnw_ref[...])
        return
    wn_ref, y_ref, o_refs = rest[0], rest[1], rest[2:]
    y_ref[...] = y
    hb = _rms(y, nnw_ref[...]).astype(BF16)
    off = 0
    for o_ref in o_refs:
        n = o_ref.shape[-1]
        o_ref[...] = jnp.dot(hb, wn_ref[:, off:off + n], preferred_element_type=F32).astype(o_ref.dtype)
        off += n


def _layer_tail(res, acts, w_outs, nw, w_gate_up, w_down, next_nw, w_next, next_splits, next_dtypes, tm):
    m, d = res.shape
    assert m % tm == 0 and w_down.shape[0] % (FFN_CHUNKS * LANES) == 0
    row = lambda a: pl.BlockSpec((tm, a.shape[1]), lambda i: (i, 0))
    resident = lambda a: pl.BlockSpec(a.shape, lambda i: (0,) * a.ndim, pipeline_mode=pl.Buffered(1))
    full = lambda a: pl.BlockSpec(a.shape, lambda i: (0,) * a.ndim)
    ins = [res, *acts, *w_outs, nw, w_gate_up, w_down, next_nw]
    in_specs = ([row(res)] + [row(a) for a in acts] + [resident(w) for w in w_outs]
                + [full(nw), resident(w_gate_up), resident(w_down), full(next_nw)])
    out_shape = [jax.ShapeDtypeStruct((m, d), F32)]
    if w_next is not None:
        ins.append(w_next)
        in_specs.append(resident(w_next))
        out_shape += [jax.ShapeDtypeStruct((m, s), dt) for s, dt in zip(next_splits, next_dtypes)]
    return pl.pallas_call(
        functools.partial(_layer_tail_body, len(acts), tuple(next_splits) if w_next is not None else ()),
        grid=(m // tm,),
        in_specs=in_specs,
        out_specs=[row(o) for o in out_shape],
        out_shape=out_shape,
        compiler_params=_params("parallel"),
        name="layer_tail",
    )(*ins)


def _conv_tile(x, w_ref, tail_scr, first_tile):
    tt = x.shape[0]

    @pl.when(first_tile)
    def _():
        tail_scr[...] = jnp.zeros_like(tail_scr)

    tail = tail_scr[...]
    row = lax.broadcasted_iota(jnp.int32, tail.shape, 0)
    y = w_ref[CONV_WIDTH - 1:CONV_WIDTH, :] * x
    for s in range(1, CONV_WIDTH):
        xs = pltpu.roll(x, s, axis=0)
        head = jnp.where(row < s, pltpu.roll(tail, s, axis=0), xs[0:CONV_TAIL])
        y = y + w_ref[CONV_WIDTH - 1 - s:CONV_WIDTH - s, :] * jnp.concatenate([head, xs[CONV_TAIL:]], axis=0)
    tail_scr[...] = x[tt - CONV_TAIL:tt]
    return y


def _in_proj_prompt_body(tiles_per_seq, x_ref, nw_ref, w_ref, cw_ref,
                         qkv_ref, z_ref, gate_ref, xr_ref, ba_ref, raw_tail_ref, tail_scr):
    tm = x_ref.shape[0]
    hb = _rms(x_ref[...], nw_ref[...]).astype(BF16)
    raw = jnp.dot(hb, w_ref[:, :GDN_QKV_W], preferred_element_type=F32)
    raw_tail_ref[0] = raw[tm - CONV_TAIL:tm]
    act = _silu(_conv_tile(raw, cw_ref, tail_scr, pl.program_id(0) % tiles_per_seq == 0))
    for h in range(GDN_HEADS):
        lo, hi = h * GDN_DK, (h + 1) * GDN_DK
        qh = act[:, lo:hi]
        qkv_ref[:, lo:hi] = qh * lax.rsqrt(jnp.sum(qh * qh, axis=-1, keepdims=True) + NORM_EPS) * GDN_DK ** -0.5
        kh = act[:, GDN_QK_W + lo:GDN_QK_W + hi]
        qkv_ref[:, GDN_QK_W + lo:GDN_QK_W + hi] = kh * lax.rsqrt(jnp.sum(kh * kh, axis=-1, keepdims=True) + NORM_EPS)
    qkv_ref[:, 2 * GDN_QK_W:] = act[:, 2 * GDN_QK_W:]
    off = GDN_QKV_W
    for o_ref in (z_ref, gate_ref, xr_ref, ba_ref):
        n = o_ref.shape[-1]
        o_ref[...] = jnp.dot(hb, w_ref[:, off:off + n], preferred_element_type=F32)
        off += n


def _in_proj_prompt(x, nw, w, splits, conv_w, seq_len, tm):
    m, d = x.shape
    assert sum(splits) == w.shape[1] and splits[0] == GDN_QKV_W and seq_len % tm == 0 and m % seq_len == 0
    row = lambda s: pl.BlockSpec((tm, s), lambda i: (i, 0))
    full = lambda a: pl.BlockSpec(a.shape, lambda i: (0,) * a.ndim)
    return pl.pallas_call(
        functools.partial(_in_proj_prompt_body, seq_len // tm),
        grid=(m // tm,),
        in_specs=[row(d), full(nw), full(w), full(conv_w)],
        out_specs=[row(s) for s in splits] + [pl.BlockSpec((1, CONV_TAIL, GDN_QKV_W), lambda i: (i, 0, 0))],
        out_shape=([jax.ShapeDtypeStruct((m, s), F32) for s in splits]
                   + [jax.ShapeDtypeStruct((m // tm, CONV_TAIL, GDN_QKV_W), F32)]),
        scratch_shapes=[pltpu.VMEM((CONV_TAIL, GDN_QKV_W), F32)],
        compiler_params=_params("arbitrary"),
        name="in_proj_prompt",
    )(x, nw, w, conv_w)


def _bmm(a, b):
    return lax.dot_general(a.astype(BF16), b.astype(BF16), (((2,), (1,)), ((0,), (0,))), preferred_element_type=F32)


def _bmm_nt(a, b):
    return lax.dot_general(a.astype(BF16), b.astype(BF16), (((2,), (2,)), ((0,), (0,))), preferred_element_type=F32)


def _gdn_prompt_body(qkv_ref, z_ref, ba_ref, alog_ref, dtb_ref, gnw_ref, o_ref, s_ref,
                     wkqd_scr, wv_scr, qk_scr, ktt_scr, gtot_scr):
    t = pl.program_id(1)
    tt = qkv_ref.shape[1]
    c = GDN_CHUNK
    nc = tt // c
    h_n = GDN_HEADS
    nb = h_n * nc

    @pl.when(t == 0)
    def _():
        s_ref[...] = jnp.zeros_like(s_ref)

    def heads(off):
        cols = [qkv_ref[0, :, off + h * GDN_DK:off + (h + 1) * GDN_DK] for h in range(h_n)]
        return jnp.stack(cols).reshape(nb, c, GDN_DK)

    q, k, v = heads(0), heads(GDN_QK_W), heads(2 * GDN_QK_W)

    ba = ba_ref[0]
    beta = _sigmoid(ba)
    g = -jnp.exp(alog_ref[...]) * _softplus(ba + dtb_ref[...])
    row = lax.broadcasted_iota(jnp.int32, ba.shape, 0) % c
    gam = g
    shift = 1
    while shift < c:
        gam = gam + jnp.where(row >= shift, pltpu.roll(gam, shift, axis=0), 0.0)
        shift *= 2
    gam_t = gam.T

    def col(a, first):
        return jnp.stack([a[:, first + h:first + h + 1] for h in range(h_n)]).reshape(nb, c, 1)

    beta_c = col(beta, 0)
    gam_c = col(gam, h_n)
    gam_r = jnp.stack([gam_t[h_n + h:h_n + h + 1, ci * c:(ci + 1) * c]
                       for h in range(h_n) for ci in range(nc)])
    gam_last = gam_c[:, c - 1:c, :]

    ri = lax.broadcasted_iota(jnp.int32, (1, c, c), 1)
    cj = lax.broadcasted_iota(jnp.int32, (1, c, c), 2)
    decay = jnp.exp(jnp.where(ri >= cj, gam_c - gam_r, -jnp.inf))
    kk = _bmm_nt(k, k)
    qk = _bmm_nt(q, k) * decay
    pw = jnp.where(ri > cj, -(beta_c * kk * decay), 0.0)
    tq = pw
    for _ in range(int(math.log2(c)) - 1):
        pw = _bmm(pw, pw)
        tq = tq + pw + _bmm(tq, pw)
    e_gam = jnp.exp(gam_c)
    rhs = jnp.concatenate([beta_c * v, (beta_c * e_gam) * k], axis=2)
    sol = rhs + _bmm(tq, rhs)
    w_key = sol[:, :, GDN_DV:]
    k_tail = k * jnp.exp(gam_last - gam_c)
    grp = lambda a: a.reshape((h_n, nc) + a.shape[1:])
    wv_scr[...] = grp(sol[:, :, :GDN_DV])
    wkqd_scr[...] = grp(jnp.concatenate([w_key, q * e_gam], axis=1).astype(BF16))
    qk_scr[...] = grp(qk.astype(BF16))
    ktt_scr[...] = grp(jnp.swapaxes(k_tail, 1, 2).astype(BF16))
    gtot_scr[...] = grp(jnp.broadcast_to(jnp.exp(gam_last), (nb, 1, GDN_DV)))

    def chunk(ci, carry):
        r0 = pl.multiple_of(ci * c, c)
        s_old = s_ref[0]
        r = _bmm(wkqd_scr[:, ci], s_old)
        u = wv_scr[:, ci] - r[:, :c]
        o = r[:, c:] + _bmm(qk_scr[:, ci], u)
        s_ref[0] = s_old * gtot_scr[:, ci] + _bmm(ktt_scr[:, ci], u)
        o = o * lax.rsqrt(jnp.mean(o * o, axis=-1, keepdims=True) + NORM_EPS) * gnw_ref[...]
        for h in range(h_n):
            lo, hi = h * GDN_DV, (h + 1) * GDN_DV
            o_ref[0, pl.ds(r0, c), lo:hi] = (o[h] * _silu(z_ref[0, pl.ds(r0, c), lo:hi])).astype(o_ref.dtype)
        return carry

    lax.fori_loop(0, nc, chunk, 0)


def _gdn_prompt(qkv, z, ba, alog_row, dtb_row, gnorm_w, tt):
    b, t, _ = qkv.shape
    assert t % tt == 0 and tt % GDN_CHUNK == 0
    c, hn, nc = GDN_CHUNK, GDN_HEADS, tt // GDN_CHUNK
    row_spec = lambda w: pl.BlockSpec((1, tt, w), lambda i, j: (i, j, 0))
    full = lambda a: pl.BlockSpec(a.shape, lambda i, j: (0,) * a.ndim)
    return pl.pallas_call(
        _gdn_prompt_body,
        grid=(b, t // tt),
        in_specs=[row_spec(GDN_QKV_W), row_spec(GDN_V_W), row_spec(LANES),
                  full(alog_row), full(dtb_row), full(gnorm_w)],
        out_specs=[row_spec(GDN_V_W),
                   pl.BlockSpec((1, hn, GDN_DK, GDN_DV), lambda i, j: (i, 0, 0, 0))],
        out_shape=[jax.ShapeDtypeStruct((b, t, GDN_V_W), BF16),
                   jax.ShapeDtypeStruct((b, hn, GDN_DK, GDN_DV), F32)],
        scratch_shapes=[pltpu.VMEM((hn, nc, 2 * c, GDN_DK), BF16),
                        pltpu.VMEM((hn, nc, c, GDN_DV), F32),
                        pltpu.VMEM((hn, nc, c, c), BF16),
                        pltpu.VMEM((hn, nc, GDN_DK, c), BF16),
                        pltpu.VMEM((hn, nc, 1, GDN_DV), F32)],
        compiler_params=_params("parallel", "arbitrary"),
        name="gdn_prompt",
    )(qkv, z, ba, alog_row, dtb_row, gnorm_w)


def _lru_gates(x, wg_ref, bg_ref, lam_ref):
    w = x.shape[1]
    gates = jnp.dot(x.astype(BF16), wg_ref[...], preferred_element_type=F32) + bg_ref[...]
    r = _sigmoid(gates[:, :w])
    i = _sigmoid(gates[:, w:])
    log_a = -LRU_C * r * _softplus(-lam_ref[...])
    a = jnp.exp(log_a)
    b = jnp.sqrt(jnp.tanh(-log_a) * (1.0 + a * a)) * (i * x)
    return a, b


def _lru_prompt_body(xr_ref, gate_ref, cw_ref, cb_ref, wg_ref, bg_ref, lam_ref, y_ref, h_ref, xp_scr):
    t = pl.program_id(1)
    tt = xr_ref.shape[1]

    @pl.when(t == 0)
    def _():
        h_ref[...] = jnp.zeros_like(h_ref)

    x = _conv_tile(xr_ref[0], cw_ref, xp_scr, t == 0) + cb_ref[...]
    a, b = _lru_gates(x, wg_ref, bg_ref, lam_ref)
    w = a.shape[1]
    ng = tt // SUBLANES
    a = a.reshape(ng, SUBLANES, w)
    b = b.reshape(ng, SUBLANES, w)
    sub = lax.broadcasted_iota(jnp.int32, (1, SUBLANES, w), 1)
    shift = 1
    while shift < SUBLANES:
        keep = sub >= shift
        a_prev = jnp.where(keep, pltpu.roll(a, shift, axis=1), 1.0)
        b_prev = jnp.where(keep, pltpu.roll(b, shift, axis=1), 0.0)
        b = a * b_prev + b
        a = a * a_prev
        shift *= 2
    h = h_ref[0]
    groups = []
    for i in range(ng):
        hs_i = b[i] + a[i] * h
        groups.append(hs_i)
        h = hs_i[SUBLANES - 1:SUBLANES, :]
    h_ref[0] = h
    y_ref[0] = (_gelu_tanh(gate_ref[0]) * jnp.concatenate(groups, axis=0)).astype(y_ref.dtype)


def _lru_prompt(xr, gate, conv_w, conv_b, w_gates, b_gates, lam, tt):
    b, t, w = xr.shape
    assert t % tt == 0
    row_spec = pl.BlockSpec((1, tt, w), lambda i, j: (i, j, 0))
    full = lambda a: pl.BlockSpec(a.shape, lambda i, j: (0,) * a.ndim)
    return pl.pallas_call(
        _lru_prompt_body,
        grid=(b, t // tt),
        in_specs=[row_spec, row_spec, full(conv_w), full(conv_b), full(w_gates), full(b_gates), full(lam)],
        out_specs=[row_spec, pl.BlockSpec((1, 1, w), lambda i, j: (i, 0, 0))],
        out_shape=[jax.ShapeDtypeStruct((b, t, w), BF16), jax.ShapeDtypeStruct((b, 1, w), F32)],
        scratch_shapes=[pltpu.VMEM((CONV_TAIL, w), F32)],
        compiler_params=_params("parallel", "arbitrary"),
        name="lru_prompt",
    )(xr, gate, conv_w, conv_b, w_gates, b_gates, lam)


def _swa_prompt_body(slope_ref, sink_ref, q_ref, kp_ref, kc_ref, vp_ref, vc_ref, o_ref, bias_scr):
    n = pl.program_id(1)
    w = WINDOW
    hd = SWA_HEAD_DIM

    @pl.when((pl.program_id(0) == 0) & (n == 0))
    def _():
        qi = lax.broadcasted_iota(jnp.int32, (w, 2 * w), 0)
        kj = lax.broadcasted_iota(jnp.int32, (w, 2 * w), 1)
        rel = qi + w - kj
        band = (rel >= 0) & (rel <= w)
        relf = rel.astype(F32)
        for hh in range(SWA_HEADS):
            bias = -(slope_ref[hh] * relf)
            bias_scr[0, hh] = jnp.where(band, bias, -jnp.inf)
            bias_scr[1, hh] = jnp.where(band & (kj >= w), bias, -jnp.inf)

    first = (n == 0).astype(jnp.int32)
    for kh in range(SWA_KV_HEADS):
        kcat = jnp.concatenate([kp_ref[0, :, kh * hd:(kh + 1) * hd], kc_ref[0, :, kh * hd:(kh + 1) * hd]],
                               axis=0).astype(BF16)
        vcat = jnp.concatenate([vp_ref[0, :, kh * hd:(kh + 1) * hd], vc_ref[0, :, kh * hd:(kh + 1) * hd]],
                               axis=0).astype(BF16)
        for hh in range(kh * SWA_GROUP, (kh + 1) * SWA_GROUP):
            s = _dot_nt(q_ref[0, :, hh * hd:(hh + 1) * hd], kcat) + bias_scr[first, hh]
            sink = sink_ref[hh]
            m = jnp.maximum(jnp.max(s, axis=-1, keepdims=True), sink)
            p = jnp.exp(s - m)
            denom = jnp.sum(p, axis=-1, keepdims=True) + jnp.exp(sink - m)
            o_ref[0, :, hh * hd:(hh + 1) * hd] = (_dot(p, vcat) / denom).astype(o_ref.dtype)


def _swa_prompt(q, k, v, slopes, sinks):
    b, t, _ = q.shape
    nb = t // WINDOW
    smem = pl.BlockSpec(memory_space=pltpu.SMEM)
    prev = pl.BlockSpec((1, WINDOW, SWA_KV_W), lambda i, j: (i, jnp.maximum(j - 1, 0), 0))
    cur = pl.BlockSpec((1, WINDOW, SWA_KV_W), lambda i, j: (i, j, 0))
    return pl.pallas_call(
        _swa_prompt_body,
        grid=(b, nb),
        in_specs=[smem, smem, pl.BlockSpec((1, WINDOW, SWA_OUT_W), lambda i, j: (i, j, 0)), prev, cur, prev, cur],
        out_specs=pl.BlockSpec((1, WINDOW, SWA_OUT_W), lambda i, j: (i, j, 0)),
        out_shape=jax.ShapeDtypeStruct((b, t, SWA_OUT_W), BF16),
        scratch_shapes=[pltpu.VMEM((2, SWA_HEADS, WINDOW, 2 * WINDOW), F32)],
        compiler_params=_params("arbitrary", "arbitrary"),
        name="swa_prompt",
    )(slopes, sinks, q, k, k, v, v)


def _gdn_sample_body(qkv_ref, buf_ref, z_ref, ba_ref, s_in_ref, cw_ref, alog_ref, dtb_ref, gnw_ref,
                     o_ref, s_out_ref, o_scr):
    bt = qkv_ref.shape[0]
    wq = GDN_QKV_W
    y = cw_ref[CONV_WIDTH - 1:CONV_WIDTH, :] * qkv_ref[...]
    for j in range(CONV_WIDTH - 1):
        y = y + cw_ref[j:j + 1, :] * buf_ref[:, j * wq:(j + 1) * wq]
    act = _silu(y)
    ba = ba_ref[...]
    beta = _sigmoid(ba)
    eg = jnp.exp(-jnp.exp(alog_ref[...]) * _softplus(ba + dtb_ref[...]))
    pad = jnp.zeros((LANES - bt, GDN_DK), F32)
    for h in range(GDN_HEADS):
        lo, hi = h * GDN_DK, (h + 1) * GDN_DK
        qh = act[:, lo:hi]
        qh = qh * lax.rsqrt(jnp.sum(qh * qh, axis=-1, keepdims=True) + NORM_EPS) * GDN_DK ** -0.5
        kh = act[:, GDN_QK_W + lo:GDN_QK_W + hi]
        kh = kh * lax.rsqrt(jnp.sum(kh * kh, axis=-1, keepdims=True) + NORM_EPS)
        vh = act[:, 2 * GDN_QK_W + lo:2 * GDN_QK_W + hi]
        qt = jnp.concatenate([qh, pad], axis=0).T
        kt = jnp.concatenate([kh, pad], axis=0).T
        for i in range(bt):
            kcol = kt[:, i:i + 1]
            qcol = qt[:, i:i + 1]
            s_dec = s_in_ref[i, h] * eg[i:i + 1, GDN_HEADS + h:GDN_HEADS + h + 1]
            pred = jnp.sum(kcol * s_dec, axis=0, keepdims=True)
            upd = beta[i:i + 1, h:h + 1] * (vh[i:i + 1, :] - pred)
            s_new = s_dec + kcol * upd
            s_out_ref[i, h] = s_new
            o_scr[i:i + 1, lo:hi] = jnp.sum(qcol * s_new, axis=0, keepdims=True)
    for h in range(GDN_HEADS):
        lo, hi = h * GDN_DV, (h + 1) * GDN_DV
        o = o_scr[:, lo:hi]
        o = o * lax.rsqrt(jnp.mean(o * o, axis=-1, keepdims=True) + NORM_EPS) * gnw_ref[...]
        o_ref[:, lo:hi] = o * _silu(z_ref[:, lo:hi])


def _gdn_sample(qkv, buf, z, ba, state, conv_w, alog_row, dtb_row, gnorm_w, bt):
    b = qkv.shape[0]
    assert b % bt == 0
    row = lambda a: pl.BlockSpec((bt, a.shape[1]), lambda i: (i, 0))
    full = lambda a: pl.BlockSpec(a.shape, lambda i: (0,) * a.ndim)
    st = pl.BlockSpec((bt, GDN_HEADS, GDN_DK, GDN_DV), lambda i: (i, 0, 0, 0))
    return pl.pallas_call(
        _gdn_sample_body,
        grid=(b // bt,),
        in_specs=[row(qkv), row(buf), row(z), row(ba), st, full(conv_w), full(alog_row), full(dtb_row), full(gnorm_w)],
        out_specs=[pl.BlockSpec((bt, GDN_V_W), lambda i: (i, 0)), st],
        out_shape=[jax.ShapeDtypeStruct((b, GDN_V_W), F32), jax.ShapeDtypeStruct(state.shape, F32)],
        scratch_shapes=[pltpu.VMEM((bt, GDN_V_W), F32)],
        compiler_params=_params("parallel"),
        name="gdn_sample",
    )(qkv, buf, z, ba, state, conv_w, alog_row, dtb_row, gnorm_w)


def _lru_sample_body(xr_ref, buf_ref, gate_ref, h0_ref, cw_ref, cb_ref, wg_ref, bg_ref, lam_ref, y_ref, h_ref):
    w = xr_ref.shape[1]
    x = cw_ref[CONV_WIDTH - 1:CONV_WIDTH, :] * xr_ref[...] + cb_ref[...]
    for j in range(CONV_WIDTH - 1):
        x = x + cw_ref[j:j + 1, :] * buf_ref[:, j * w:(j + 1) * w]
    a, b = _lru_gates(x, wg_ref, bg_ref, lam_ref)
    h = a * h0_ref[...] + b
    h_ref[...] = h
    y_ref[...] = _gelu_tanh(gate_ref[...]) * h


def _lru_sample(xr, buf, gate, h0, conv_w, conv_b, w_gates, b_gates, lam):
    b, w = xr.shape
    return pl.pallas_call(
        _lru_sample_body,
        out_shape=[jax.ShapeDtypeStruct((b, w), F32), jax.ShapeDtypeStruct((b, w), F32)],
        compiler_params=pltpu.CompilerParams(vmem_limit_bytes=VMEM_LIMIT_BYTES),
        name="lru_sample",
    )(xr, buf, gate, h0, conv_w, conv_b, w_gates, b_gates, lam)


def _swa_sample_body(q_ref, kn_ref, vn_ref, kc_ref, vc_ref, slope_ref, sink_ref, o_ref, ko_ref, vo_ref):
    lb = kc_ref.shape[1]
    hd = SWA_HEAD_DIM
    head = lax.broadcasted_iota(jnp.int32, (1, SWA_HEADS, SWA_KV_W), 1)
    col = lax.broadcasted_iota(jnp.int32, (1, SWA_HEADS, SWA_KV_W), 2)
    own = (head // SWA_GROUP) == (col // hd)
    rel = (lb - lax.broadcasted_iota(jnp.int32, (1, 1, lb), 2)).astype(F32)
    sink = sink_ref[...]
    kn, vn = kn_ref[...], vn_ref[...]
    q = q_ref[...]
    qx = jnp.where(own, jnp.concatenate([q] * SWA_KV_HEADS, axis=2), 0.0)
    s = _bmm_nt(qx, kc_ref[...]) - slope_ref[...] * rel
    s_new = jnp.sum(qx * kn, axis=-1, keepdims=True)
    m = jnp.maximum(jnp.maximum(jnp.max(s, axis=-1, keepdims=True), s_new), sink)
    p = jnp.exp(s - m)
    p_new = jnp.exp(s_new - m)
    denom = jnp.sum(p, axis=-1, keepdims=True) + p_new + jnp.exp(sink - m)
    ox = jnp.where(own, (_bmm(p, vc_ref[...]) + p_new * vn) / denom, 0.0)
    o = ox[:, :, 0:hd]
    for kh in range(1, SWA_KV_HEADS):
        o = o + ox[:, :, kh * hd:(kh + 1) * hd]
    o_ref[...] = o
    ko_ref[:, 0:lb - 1, :] = kc_ref[:, 1:lb, :]
    ko_ref[:, lb - 1:lb, :] = kn
    vo_ref[:, 0:lb - 1, :] = vc_ref[:, 1:lb, :]
    vo_ref[:, lb - 1:lb, :] = vn


def _swa_sample(q, kn, vn, kc, vc, slopes, sinks, bt):
    b, lb, _ = kc.shape
    assert b % bt == 0 and lb <= WINDOW
    row = pl.BlockSpec((bt, 1, SWA_KV_W), lambda i: (i, 0, 0))
    cache = pl.BlockSpec((bt, lb, SWA_KV_W), lambda i: (i, 0, 0))
    hq = pl.BlockSpec((bt, SWA_HEADS, SWA_HEAD_DIM), lambda i: (i, 0, 0))
    col = pl.BlockSpec((1, SWA_HEADS, 1), lambda i: (0, 0, 0))
    return pl.pallas_call(
        _swa_sample_body,
        grid=(b // bt,),
        in_specs=[hq, row, row, cache, cache, col, col],
        out_specs=[hq, cache, cache],
        out_shape=[jax.ShapeDtypeStruct((b, SWA_HEADS, SWA_HEAD_DIM), F32),
                   jax.ShapeDtypeStruct(kc.shape, F32), jax.ShapeDtypeStruct(vc.shape, F32)],
        compiler_params=_params("parallel"),
        name="swa_sample",
    )(q, kn, vn, kc, vc, slopes, sinks)


def _block_diag(w):
    n, bi, bj = w.shape
    eye = jnp.eye(n, dtype=w.dtype)
    return (eye[:, None, :, None] * w[:, :, None, :]).reshape(n * bi, n * bj)


def _prepare(prm):
    w_in = prm['w_in_ab'][0]
    c1 = GDN_QKV_W
    c2 = c1 + GDN_V_W
    c3 = c2 + 2 * GDN_HEADS
    lru_w = (w_in.shape[1] - c3) // 2
    d = w_in.shape[0]
    w_in_r = jnp.concatenate(
        [w_in[:, :c2], w_in[:, c3:], w_in[:, c2:c3], jnp.zeros((d, LANES - 2 * GDN_HEADS), w_in.dtype)], axis=1)
    lane_pad = lambda v, at: jnp.zeros((1, LANES), F32).at[0, at:at + v.shape[0]].set(v)
    slopes = jnp.exp2(-8.0 * jnp.arange(1, SWA_HEADS + 1, dtype=F32) / SWA_HEADS)
    return dict(
        w_in=w_in_r.astype(BF16),
        in_splits=(GDN_QKV_W, GDN_V_W, lru_w, lru_w, LANES),
        conv_gdn_w=prm['conv_gdn_w'][0],
        alog_row=lane_pad(prm['gdn_a_log'][0], GDN_HEADS),
        dtb_row=lane_pad(prm['gdn_dt_bias'][0], GDN_HEADS),
        gnorm_w=prm['gdn_norm_w'][0][None, :],
        conv_lru_w=prm['conv_lru_w'][0],
        conv_lru_b=prm['conv_lru_b'][0][None, :],
        w_gates=jnp.concatenate([_block_diag(prm['lru_wa'][0]), _block_diag(prm['lru_wx'][0])], axis=1).astype(BF16),
        b_gates=jnp.concatenate([prm['lru_ba'][0], prm['lru_bx'][0]])[None, :],
        lam=prm['lru_lambda'][0][None, :],
        w_out_gdn=prm['w_out_ab'][0][:GDN_V_W].astype(BF16),
        w_out_lru=prm['w_out_ab'][0][GDN_V_W:].astype(BF16),
        w_qkv_c=prm['w_qkv_c'][0].astype(BF16).at[:, :SWA_OUT_W].multiply(SWA_HEAD_DIM ** -0.5),
        w_out_c=prm['w_out_c'][0].astype(BF16),
        slopes=slopes,
        sinks=prm['sinks_c'][0],
        w_gate_up=[w.astype(BF16) for w in prm['w_gate_up']],
        w_down=[w.astype(BF16) for w in prm['w_down']],
        norm_mix=[w[None, :] for w in prm['norm_mix']],
        norm_ffn=[w[None, :] for w in prm['norm_ffn']],
        norm_final=prm['norm_final'][None, :],
    )


def _tile(m, pref):
    return pref if m % pref == 0 else m


def _trunk_prompt(x, p):
    b, t, d = x.shape
    m = b * t
    tm = _tile(m, 512)
    tt = _tile(t, 512)
    x0 = x.reshape(m, d)
    qkv, z, gate, xr, ba, raw_tail = _in_proj_prompt(x0, p['norm_mix'][0], p['w_in'], p['in_splits'],
                                                     p['conv_gdn_w'], t, tt)
    r3 = lambda a: a.reshape(b, t, a.shape[-1])
    o_gdn, s_fin = _gdn_prompt(r3(qkv), r3(z), r3(ba), p['alog_row'], p['dtb_row'], p['gnorm_w'], tt)
    y_lru, h_fin = _lru_prompt(r3(xr), r3(gate), p['conv_lru_w'], p['conv_lru_b'], p['w_gates'], p['b_gates'],
                               p['lam'], tt)
    x2, q, k, v = _layer_tail(x0, [o_gdn.reshape(m, -1), y_lru.reshape(m, -1)], [p['w_out_gdn'], p['w_out_lru']],
                              p['norm_ffn'][0], p['w_gate_up'][0], p['w_down'][0], p['norm_mix'][1], p['w_qkv_c'],
                              SWA_SPLITS, SWA_DTYPES, tm)
    att = _swa_prompt(r3(q), r3(k), r3(v), p['slopes'], p['sinks'])
    y, = _layer_tail(x2, [att.reshape(m, -1)], [p['w_out_c']], p['norm_ffn'][1], p['w_gate_up'][1], p['w_down'][1],
                     p['norm_final'], None, None, None, tm)
    keep = CONV_WIDTH - 1
    gcb = raw_tail.reshape(b, t // tt, CONV_TAIL, -1)[:, -1, CONV_TAIL - keep:]
    kv_tail = lambda a: r3(a)[:, t - WINDOW:].reshape(b, WINDOW, SWA_KV_HEADS, SWA_HEAD_DIM)
    return (y.reshape(b, t, d), s_fin[None], gcb[None], h_fin.reshape(b, -1)[None],
            r3(xr)[:, t - keep:][None], kv_tail(k)[None], kv_tail(v)[None])


def _trunk_sample(x, gdn_s, gdn_cb, lru_h, lru_cb, swa_k, swa_v, p):
    b, t, d = x.shape
    assert t == 1
    x0 = x.reshape(b, d)
    tm = _tile(b, 128)
    bt = _tile(b, 8)
    qkv, z, gate, xr, ba = _norm_matmul(x0, p['norm_mix'][0], p['w_in'], p['in_splits'], tm)
    o_gdn, s_new = _gdn_sample(qkv, gdn_cb[0].reshape(b, -1), z, ba, gdn_s[0], p['conv_gdn_w'], p['alog_row'],
                               p['dtb_row'], p['gnorm_w'], bt)
    y_lru, h_new = _lru_sample(xr, lru_cb[0].reshape(b, -1), gate, lru_h[0], p['conv_lru_w'], p['conv_lru_b'],
                               p['w_gates'], p['b_gates'], p['lam'])
    x2, q, k, v = _layer_tail(x0, [o_gdn, y_lru], [p['w_out_gdn'], p['w_out_lru']], p['norm_ffn'][0],
                              p['w_gate_up'][0], p['w_down'][0], p['norm_mix'][1], p['w_qkv_c'],
                              SWA_SPLITS, SWA_DTYPES, tm)
    lb = swa_k.shape[2]
    att, k_new, v_new = _swa_sample(q.reshape(b, SWA_HEADS, SWA_HEAD_DIM), k[:, None, :], v[:, None, :],
                                    swa_k[0].reshape(b, lb, SWA_KV_W), swa_v[0].reshape(b, lb, SWA_KV_W),
                                    p['slopes'].reshape(1, -1, 1), p['sinks'].reshape(1, -1, 1), bt)
    y, = _layer_tail(x2, [att.reshape(b, SWA_OUT_W)], [p['w_out_c']], p['norm_ffn'][1], p['w_gate_up'][1],
                     p['w_down'][1], p['norm_final'], None, None, None, tm)
    gcb = jnp.concatenate([gdn_cb[0][:, 1:], qkv[:, None, :]], axis=1)
    lcb = jnp.concatenate([lru_cb[0][:, 1:], xr[:, None, :]], axis=1)
    r5 = lambda a: a.reshape(b, lb, SWA_KV_HEADS, SWA_HEAD_DIM)
    return (y.reshape(b, 1, d), s_new[None], gcb[None], h_new[None], lcb[None], r5(k_new)[None], r5(v_new)[None])


def kernel(x_prompt, x_sample, state_gdn, state_gdn_conv, state_lru, state_lru_conv, cache_swa_k, cache_swa_v,
           norm_mix, norm_ffn, norm_final, w_in_ab, conv_gdn_w, gdn_a_log, gdn_dt_bias, gdn_norm_w,
           conv_lru_w, conv_lru_b, lru_wa, lru_ba, lru_wx, lru_bx, lru_lambda, w_out_ab, w_qkv_c, w_out_c,
           sinks_c, w_gate_up, w_down):
    assert norm_mix.shape[0] == 2 and w_in_ab.shape[0] == 1 and w_qkv_c.shape[0] == 1
    p = _prepare(dict(norm_mix=norm_mix, norm_ffn=norm_ffn, norm_final=norm_final, w_in_ab=w_in_ab,
                      conv_gdn_w=conv_gdn_w, gdn_a_log=gdn_a_log, gdn_dt_bias=gdn_dt_bias, gdn_norm_w=gdn_norm_w,
                      conv_lru_w=conv_lru_w, conv_lru_b=conv_lru_b, lru_wa=lru_wa, lru_ba=lru_ba, lru_wx=lru_wx,
                      lru_bx=lru_bx, lru_lambda=lru_lambda, w_out_ab=w_out_ab, w_qkv_c=w_qkv_c, w_out_c=w_out_c,
                      sinks_c=sinks_c, w_gate_up=w_gate_up, w_down=w_down))
    y_p, p_gdn, p_gcb, p_lru, p_lcb, p_k, p_v = _trunk_prompt(x_prompt, p)
    y_s, s_gdn, s_gcb, s_lru, s_lcb, s_k, s_v = _trunk_sample(
        x_sample, state_gdn, state_gdn_conv, state_lru, state_lru_conv, cache_swa_k, cache_swa_v, p)
    return (y_p, y_s, p_gdn, p_gcb, p_lru, p_lcb, p_k, p_v, s_gdn, s_gcb, s_lru, s_lcb, s_k, s_v)
```
